```python
import jax, jax.numpy as jnp
from jax import lax
import numpy as np

D_MODEL = 1024
BATCH = 8
SEQ = 4096
DEPTH = 2

CHUNK = 64
EPS = 1e-6
D_MIX = D_MODEL
A_WIDTH = D_MIX // 4
A_HEADS = 4
A_HEAD_DIM = A_WIDTH // A_HEADS
SGU_BLOCK = 128
B_WIDTH = D_MIX // 4
POOL_WINDOWS = (2, 4, 8, 16)
B_GROUPS = len(POOL_WINDOWS)
B_GROUP_DIM = B_WIDTH // B_GROUPS
C_WIDTH = D_MIX - A_WIDTH - B_WIDTH
C_HEADS = 8
C_HEAD_DIM = C_WIDTH // C_HEADS
CONV_WIDTH = 4
LRU_C = 8.0
SPLIT_SIZES = (A_WIDTH, A_WIDTH, A_WIDTH, B_WIDTH, B_WIDTH, C_WIDTH, C_WIDTH)
D_IN = sum(SPLIT_SIZES)
SPLIT_POINTS = tuple(int(v) for v in np.cumsum(SPLIT_SIZES)[:-1])

kernel_name = 'hybrid_sgu_pool_rglru_parallel_heads'


def rmsnorm(x, g):
    xf = x.astype(jnp.float32)
    y = xf * lax.rsqrt(jnp.mean(xf * xf, axis=-1, keepdims=True) + EPS) * g.astype(jnp.float32)
    return y.astype(x.dtype)


def sgu_mixer(u, v, norm_g, w_s, b_s):
    bsz, s, _ = v.shape
    v = rmsnorm(v, norm_g)
    vb = v.reshape(bsz, s // SGU_BLOCK, SGU_BLOCK, A_HEADS, A_HEAD_DIM)
    chunk_id = jnp.arange(SGU_BLOCK) // CHUNK
    mask = (chunk_id[None, :] <= chunk_id[:, None]).astype(w_s.dtype)
    w = w_s * mask[None]
    y = jnp.einsum('hij,bnjhd->bnihd', w, vb) + b_s.T[None, None, :, :, None]
    return u * y.reshape(bsz, s, A_WIDTH)


def pool_mixer(xb, w_pool, b_pool, scale):
    bsz, s, _ = xb.shape
    xf = xb.astype(jnp.float32)
    cs = jnp.concatenate([jnp.zeros((bsz, 1, B_WIDTH), jnp.float32), jnp.cumsum(xf, axis=1)], axis=1)
    t = jnp.arange(s)
    outs = []
    for g, win in enumerate(POOL_WINDOWS):
        sl = slice(g * B_GROUP_DIM, (g + 1) * B_GROUP_DIM)
        c = cs[..., sl]
        upper = c[:, 1:]
        lower = jnp.pad(c[:, :s + 1 - win], ((0, 0), (win - 1, 0), (0, 0)))
        count = jnp.minimum(t + 1, win).astype(jnp.float32)[None, :, None]
        outs.append((upper - lower) / count - xf[..., sl])
    p = jnp.stack(outs, axis=2).astype(xb.dtype)
    y = jnp.einsum('bsgi,gio->bsgo', p, w_pool).reshape(bsz, s, B_WIDTH) + b_pool
    return y * scale


def rglru_mixer(xc, conv_w, conv_b, w_a, b_a, w_x, b_x, lam):
    bsz, s, _ = xc.shape
    xp = jnp.pad(xc, ((0, 0), (CONV_WIDTH - 1, 0), (0, 0)))
    conv = conv_b + sum(xp[:, k:k + s] * conv_w[k] for k in range(CONV_WIDTH))
    xh = conv.reshape(bsz, s, C_HEADS, C_HEAD_DIM)
    r = jax.nn.sigmoid(jnp.einsum('bshi,hio->bsho', xh, w_a).reshape(bsz, s, C_WIDTH) + b_a)
    i = jax.nn.sigmoid(jnp.einsum('bshi,hio->bsho', xh, w_x).reshape(bsz, s, C_WIDTH) + b_x)
    log_a = -LRU_C * r.astype(jnp.float32) * jax.nn.softplus(-lam.astype(jnp.float32))
    a = jnp.exp(log_a)
    mult = jnp.sqrt(-jnp.expm1(2.0 * log_a))
    bterm = mult * (i * conv).astype(jnp.float32)

    def combine(left, right):
        a1, b1 = left
        a2, b2 = right
        return a1 * a2, a2 * b1 + b2

    _, h = lax.associative_scan(combine, (a, bterm), axis=1)
    return h.astype(xc.dtype)


def setup_inputs(seed: int = 0) -> dict:
    key = jax.random.key(seed)
    ks = jax.random.split(key, 20)
    f32 = jnp.float32
    x = jax.random.normal(ks[0], (BATCH, SEQ, D_MODEL), f32)
    norm_g = 1.0 + 0.1 * jax.random.normal(ks[1], (DEPTH, D_MODEL), f32)
    w_in = jax.random.normal(ks[2], (DEPTH, D_MODEL, D_IN), f32) * D_MODEL ** -0.5
    sgu_norm_g = 1.0 + 0.1 * jax.random.normal(ks[3], (DEPTH, A_WIDTH), f32)
    sgu_w = jax.random.normal(ks[4], (DEPTH, A_HEADS, SGU_BLOCK, SGU_BLOCK), f32) * SGU_BLOCK ** -0.5
    sgu_b = 1.0 + 0.1 * jax.random.normal(ks[5], (DEPTH, A_HEADS, SGU_BLOCK), f32)
    pool_w = jax.random.normal(ks[6], (DEPTH, B_GROUPS, B_GROUP_DIM, B_GROUP_DIM), f32) * B_GROUP_DIM ** -0.5
    pool_b = 0.02 * jax.random.normal(ks[7], (DEPTH, B_WIDTH), f32)
    pool_scale = 1.0 + 0.1 * jax.random.normal(ks[8], (DEPTH, B_WIDTH), f32)
    conv_w = jax.random.normal(ks[9], (DEPTH, CONV_WIDTH, C_WIDTH), f32) * CONV_WIDTH ** -0.5
    conv_b = 0.02 * jax.random.normal(ks[10], (DEPTH, C_WIDTH), f32)
    lru_wa = jax.random.normal(ks[11], (DEPTH, C_HEADS, C_HEAD_DIM, C_HEAD_DIM), f32) * C_HEAD_DIM ** -0.5
    lru_ba = 0.02 * jax.random.normal(ks[12], (DEPTH, C_WIDTH), f32)
    lru_wx = jax.random.normal(ks[13], (DEPTH, C_HEADS, C_HEAD_DIM, C_HEAD_DIM), f32) * C_HEAD_DIM ** -0.5
    lru_bx = 0.02 * jax.random.normal(ks[14], (DEPTH, C_WIDTH), f32)
    u = jax.random.uniform(ks[15], (DEPTH, C_WIDTH), f32, 0.9, 0.999)
    a0 = u ** (1.0 / LRU_C)
    lru_lambda = jnp.log(a0) - jnp.log1p(-a0)
    branch_norm_g = 1.0 + 0.1 * jax.random.normal(ks[16], (DEPTH, D_MIX), f32)
    w_out = jax.random.normal(ks[17], (DEPTH, D_MIX, D_MODEL), f32) * D_MIX ** -0.5
    final_g = 1.0 + 0.1 * jax.random.normal(ks[18], (D_MODEL,), f32)
    return {'x': x, 'norm_g': norm_g, 'w_in': w_in, 'sgu_norm_g': sgu_norm_g, 'sgu_w': sgu_w,
            'sgu_b': sgu_b, 'pool_w': pool_w, 'pool_b': pool_b, 'pool_scale': pool_scale,
            'conv_w': conv_w, 'conv_b': conv_b, 'lru_wa': lru_wa, 'lru_ba': lru_ba,
            'lru_wx': lru_wx, 'lru_bx': lru_bx, 'lru_lambda': lru_lambda,
            'branch_norm_g': branch_norm_g, 'w_out': w_out, 'final_g': final_g}


def reference(x, norm_g, w_in, sgu_norm_g, sgu_w, sgu_b, pool_w, pool_b, pool_scale,
              conv_w, conv_b, lru_wa, lru_ba, lru_wx, lru_bx, lru_lambda,
              branch_norm_g, w_out, final_g):
    for l in range(DEPTH):
        h = rmsnorm(x, norm_g[l])
        proj = jnp.einsum('bsd,de->bse', h, w_in[l])
        a_u, a_v, a_z, b_x, b_z, c_x, c_z = jnp.split(proj, SPLIT_POINTS, axis=-1)
        ya = sgu_mixer(a_u, a_v, sgu_norm_g[l], sgu_w[l], sgu_b[l])
        yb = pool_mixer(b_x, pool_w[l], pool_b[l], pool_scale[l])
        yc = rglru_mixer(c_x, conv_w[l], conv_b[l], lru_wa[l], lru_ba[l], lru_wx[l], lru_bx[l], lru_lambda[l])
        g = branch_norm_g[l]
        ya = rmsnorm(ya, g[:A_WIDTH]) * jax.nn.silu(a_z)
        yb = rmsnorm(yb, g[A_WIDTH:A_WIDTH + B_WIDTH]) * jax.nn.silu(b_z)
        yc = rmsnorm(yc, g[A_WIDTH + B_WIDTH:]) * jax.nn.silu(c_z)
        y = jnp.concatenate([ya, yb, yc], axis=-1)
        x = x + jnp.einsum('bse,ed->bsd', y, w_out[l])
    return rmsnorm(x, final_g)
```

```python
import functools

import jax
import jax.numpy as jnp
from jax import lax
from jax.experimental import pallas as pl
from jax.experimental.pallas import tpu as pltpu

D_MODEL = 1024
DEPTH = 2
CHUNK = 64
EPS = 1e-6
A_WIDTH = 256
A_HEADS = 4
A_HEAD_DIM = 64
SGU_BLOCK = 128
B_WIDTH = 256
POOL_WINDOWS = (2, 4, 8, 16)
B_GROUP_DIM = 64
C_WIDTH = 512
C_HEADS = 8
C_HEAD_DIM = 64
CONV_WIDTH = 4
LRU_C = 8.0
D_IN = 3 * A_WIDTH + 2 * B_WIDTH + 2 * C_WIDTH

SUBLANES = 8
LANES = 128
TOK = 512
POOL_HALO = 16
CONV_HALO = 8
C_HALF = C_WIDTH // 2
VMEM_LIMIT_BYTES = 48 * 1024 * 1024

_BF16 = jnp.bfloat16
_F32 = jnp.float32


def _rmsnorm(x, g):
    ms = jnp.mean(x * x, axis=-1, keepdims=True)
    return x * lax.rsqrt(ms + EPS) * g


def _silu(z):
    return z * jax.nn.sigmoid(z)


def _scan_half(a, b, h0, h_ref, col):
    groups = TOK // SUBLANES
    a3 = a.reshape(groups, SUBLANES, C_HALF)
    b3 = b.reshape(groups, SUBLANES, C_HALF)
    sub = lax.broadcasted_iota(jnp.int32, (1, SUBLANES, C_HALF), 1)
    k = 1
    while k < SUBLANES:
        m = sub >= k
        a_sh = jnp.where(m, pltpu.roll(a3, k, axis=1), 1.0)
        b_sh = jnp.where(m, pltpu.roll(b3, k, axis=1), 0.0)
        b3 = a3 * b_sh + b3
        a3 = a3 * a_sh
        k *= 2
    hb = h0
    for g in range(groups):
        hv = a3[g] * hb + b3[g]
        h_ref[g * SUBLANES:(g + 1) * SUBLANES, col:col + C_HALF] = hv
        hb = jnp.broadcast_to(hv[SUBLANES - 1:SUBLANES, :], (SUBLANES, C_HALF))
    return hb


def _layer_kernel(x_ref, norm_g_ref, w_in_ref, sgu_g_ref, sgu_w_ref, sgu_bias_ref,
                  pool_w_ref, pool_b_ref, pool_scale_ref, conv_w_ref, conv_b_ref,
                  gate_w_ref, b_a_ref, b_x_ref, lam_ref, bng_ref, w_out_ref, final_g_ref,
                  o_ref,
                  xb_ext, s2_ext, s4_ext, s8_ext, cx_ext, h_carry, h_buf, y_buf,
                  *, is_last):
    c = pl.program_id(1)

    @pl.when(c == 0)
    def _reset():
        xb_ext[0:POOL_HALO, :] = jnp.zeros((POOL_HALO, B_WIDTH), _F32)
        s2_ext[0:POOL_HALO, :] = jnp.zeros((POOL_HALO, B_WIDTH), _F32)
        s4_ext[0:POOL_HALO, :] = jnp.zeros((POOL_HALO, LANES), _F32)
        s8_ext[0:POOL_HALO, :] = jnp.zeros((POOL_HALO, LANES), _F32)
        cx_ext[0:CONV_HALO, :] = jnp.zeros((CONV_HALO, C_WIDTH), _F32)
        h_carry[...] = jnp.zeros((SUBLANES, C_WIDTH), _F32)

    x = x_ref[...]
    hn = _rmsnorm(x, norm_g_ref[...]).astype(_BF16)
    bng = bng_ref[...]

    off = 0
    proj_a = jnp.dot(hn, w_in_ref[:, off:off + 3 * A_WIDTH], preferred_element_type=_F32)
    a_u = proj_a[:, 0:A_WIDTH]
    a_v = proj_a[:, A_WIDTH:2 * A_WIDTH]
    a_z = proj_a[:, 2 * A_WIDTH:3 * A_WIDTH]
    v = _rmsnorm(a_v, sgu_g_ref[...]).astype(_BF16)
    ci = lax.broadcasted_iota(jnp.int32, (SGU_BLOCK, SGU_BLOCK), 0) // CHUNK
    cj = lax.broadcasted_iota(jnp.int32, (SGU_BLOCK, SGU_BLOCK), 1) // CHUNK
    mask = (cj <= ci).astype(_F32)
    wm = jnp.concatenate([(sgu_w_ref[h] * mask).astype(_BF16) for h in range(A_HEADS)], axis=0)
    lane_a = lax.broadcasted_iota(jnp.int32, (1, A_WIDTH), 1)
    ys = []
    for n in range(TOK // SGU_BLOCK):
        r = jnp.dot(wm, v[n * SGU_BLOCK:(n + 1) * SGU_BLOCK, :], preferred_element_type=_F32)
        yn = r[(A_HEADS - 1) * SGU_BLOCK:, :]
        for h in range(A_HEADS - 2, -1, -1):
            yn = jnp.where(lane_a < (h + 1) * A_HEAD_DIM, r[h * SGU_BLOCK:(h + 1) * SGU_BLOCK, :], yn)
        ys.append(yn + sgu_bias_ref[...])
    ya = a_u * jnp.concatenate(ys, axis=0)
    ya = _rmsnorm(ya, bng[:, 0:A_WIDTH]) * _silu(a_z)
    y_buf[:, 0:A_WIDTH] = ya.astype(_BF16)

    off += 3 * A_WIDTH
    proj_b = jnp.dot(hn, w_in_ref[:, off:off + 2 * B_WIDTH], preferred_element_type=_F32)
    bx = proj_b[:, 0:B_WIDTH]
    b_z = proj_b[:, B_WIDTH:2 * B_WIDTH]
    xb_ext[POOL_HALO:POOL_HALO + TOK, :] = bx
    s2 = bx + xb_ext[pl.ds(POOL_HALO - 1, TOK), :]
    s2_ext[POOL_HALO:POOL_HALO + TOK, :] = s2
    s4 = s2 + s2_ext[pl.ds(POOL_HALO - 2, TOK), :]
    s4_hi = s4[:, LANES:]
    s4_ext[POOL_HALO:POOL_HALO + TOK, :] = s4_hi
    s8 = s4_hi + s4_ext[pl.ds(POOL_HALO - 4, TOK), :]
    s8_ext[POOL_HALO:POOL_HALO + TOK, :] = s8
    s16 = s8 + s8_ext[pl.ds(POOL_HALO - 8, TOK), :]
    lane = lax.broadcasted_iota(jnp.int32, (1, LANES), 1)
    wsum = jnp.concatenate([jnp.where(lane < B_GROUP_DIM, s2[:, :LANES], s4[:, :LANES]),
                            jnp.where(lane < B_GROUP_DIM, s8, s16)], axis=1)
    lane_b = lax.broadcasted_iota(jnp.int32, (1, B_WIDTH), 1) // B_GROUP_DIM
    win = jnp.where(lane_b == 0, POOL_WINDOWS[0],
                    jnp.where(lane_b == 1, POOL_WINDOWS[1],
                              jnp.where(lane_b == 2, POOL_WINDOWS[2], POOL_WINDOWS[3])))
    inv_win = 1.0 / win.astype(_F32)
    t_head = c * TOK + lax.broadcasted_iota(jnp.int32, (POOL_HALO, B_WIDTH), 0)
    cnt_head = jnp.minimum(t_head + 1, win).astype(_F32)
    mean = jnp.concatenate([wsum[0:POOL_HALO] / cnt_head, wsum[POOL_HALO:] * inv_win], axis=0)
    p = (mean - bx).astype(_BF16)
    yb = jnp.dot(p, pool_w_ref[...], preferred_element_type=_F32) + pool_b_ref[...]
    yb = yb * pool_scale_ref[...]
    yb = _rmsnorm(yb, bng[:, A_WIDTH:A_WIDTH + B_WIDTH]) * _silu(b_z)
    y_buf[:, A_WIDTH:A_WIDTH + B_WIDTH] = yb.astype(_BF16)
    xb_ext[0:POOL_HALO, :] = xb_ext[TOK:TOK + POOL_HALO, :]
    s2_ext[0:POOL_HALO, :] = s2_ext[TOK:TOK + POOL_HALO, :]
    s4_ext[0:POOL_HALO, :] = s4_ext[TOK:TOK + POOL_HALO, :]
    s8_ext[0:POOL_HALO, :] = s8_ext[TOK:TOK + POOL_HALO, :]

    off += 2 * B_WIDTH
    proj_c = jnp.dot(hn, w_in_ref[:, off:off + 2 * C_WIDTH], preferred_element_type=_F32)
    cx = proj_c[:, 0:C_WIDTH]
    c_z = proj_c[:, C_WIDTH:2 * C_WIDTH]
    cx_ext[CONV_HALO:CONV_HALO + TOK, :] = cx
    cw = conv_w_ref[...]
    conv = conv_b_ref[...] + cx * cw[CONV_WIDTH - 1:CONV_WIDTH, :]
    for k in range(CONV_WIDTH - 1):
        shift = CONV_WIDTH - 1 - k
        conv = conv + cx_ext[pl.ds(CONV_HALO - shift, TOK), :] * cw[k:k + 1, :]
    cx_ext[0:CONV_HALO, :] = cx_ext[TOK:TOK + CONV_HALO, :]
    conv_bf = conv.astype(_BF16)
    neg_c_softplus = -LRU_C * jax.nn.softplus(-lam_ref[...])
    hc = h_carry[...]
    ssq = jnp.zeros((TOK, 1), _F32)
    for d in range(2):
        lo = d * C_HALF
        gates = jnp.dot(conv_bf[:, lo:lo + C_HALF], gate_w_ref[d], preferred_element_type=_F32)
        r = jax.nn.sigmoid(gates[:, 0:C_HALF] + b_a_ref[:, lo:lo + C_HALF])
        i = jax.nn.sigmoid(gates[:, C_HALF:] + b_x_ref[:, lo:lo + C_HALF])
        log_a = r * neg_c_softplus[:, lo:lo + C_HALF]
        a = jnp.exp(log_a)
        mult = jnp.sqrt(-jnp.tanh(log_a) * (a * a + 1.0))
        bterm = mult * (i * conv[:, lo:lo + C_HALF])
        hb = _scan_half(a, bterm, hc[:, lo:lo + C_HALF], h_buf, lo)
        h_carry[:, lo:lo + C_HALF] = hb
    hs = h_buf[...]
    yc = _rmsnorm(hs, bng[:, A_WIDTH + B_WIDTH:]) * _silu(c_z)
    y_buf[:, A_WIDTH + B_WIDTH:] = yc.astype(_BF16)

    out = x + jnp.dot(y_buf[...], w_out_ref[...], preferred_element_type=_F32)
    if is_last:
        out = _rmsnorm(out, final_g_ref[...])
    o_ref[...] = out


def _full(shape):
    return pl.BlockSpec(shape, lambda b, c: (0,) * len(shape))


def _block_diag(w):
    n, d, _ = w.shape
    eye = jnp.eye(n, dtype=w.dtype)
    return (eye[:, None, :, None] * w[:, :, None, :]).reshape(n * d, n * d)


def _layer(x, norm_g, w_in, sgu_g, sgu_w, sgu_b, pool_w, pool_b, pool_scale, conv_w, conv_b,
           lru_wa, lru_ba, lru_wx, lru_bx, lam, bng, w_out, final_g, *, is_last, name):
    bsz, seq, _ = x.shape
    assert seq % TOK == 0 and TOK % SGU_BLOCK == 0
    row = lambda a: a.reshape(1, -1).astype(_F32)
    heads_per_half = C_HEADS // 2
    gate_w = jnp.stack([
        jnp.concatenate([_block_diag(lru_wa[d * heads_per_half:(d + 1) * heads_per_half]),
                         _block_diag(lru_wx[d * heads_per_half:(d + 1) * heads_per_half])], axis=1)
        for d in range(2)]).astype(_BF16)
    sgu_bias = jnp.repeat(sgu_b.T, A_HEAD_DIM, axis=1).astype(_F32)
    sgu_bias = sgu_bias
    operands = (
        x, row(norm_g), w_in.astype(_BF16), row(sgu_g), sgu_w.astype(_F32), sgu_bias,
        _block_diag(pool_w).astype(_BF16), row(pool_b), row(pool_scale),
        conv_w.astype(_F32), row(conv_b), gate_w, row(lru_ba), row(lru_bx), row(lam),
        row(bng), w_out.astype(_BF16), row(final_g),
    )
    in_specs = [pl.BlockSpec((None, TOK, D_MODEL), lambda b, c: (b, c, 0))]
    in_specs += [_full(op.shape) for op in operands[1:]]
    return pl.pallas_call(
        functools.partial(_layer_kernel, is_last=is_last),
        out_shape=jax.ShapeDtypeStruct(x.shape, x.dtype),
        grid=(bsz, seq // TOK),
        in_specs=in_specs,
        out_specs=pl.BlockSpec((None, TOK, D_MODEL), lambda b, c: (b, c, 0)),
        scratch_shapes=[
            pltpu.VMEM((TOK + POOL_HALO, B_WIDTH), _F32),
            pltpu.VMEM((TOK + POOL_HALO, B_WIDTH), _F32),
            pltpu.VMEM((TOK + POOL_HALO, LANES), _F32),
            pltpu.VMEM((TOK + POOL_HALO, LANES), _F32),
            pltpu.VMEM((TOK + CONV_HALO, C_WIDTH), _F32),
            pltpu.VMEM((SUBLANES, C_WIDTH), _F32),
            pltpu.VMEM((TOK, C_WIDTH), _F32),
            pltpu.VMEM((TOK, D_MODEL), _BF16),
        ],
        compiler_params=pltpu.CompilerParams(
            dimension_semantics=("arbitrary", "arbitrary"),
            vmem_limit_bytes=VMEM_LIMIT_BYTES,
        ),
        name=name,
    )(*operands)


def kernel(x, norm_g, w_in, sgu_norm_g, sgu_w, sgu_b, pool_w, pool_b, pool_scale, conv_w, conv_b,
           lru_wa, lru_ba, lru_wx, lru_bx, lru_lambda, branch_norm_g, w_out, final_g):
    for l in range(DEPTH):
        x = _layer(x, norm_g[l], w_in[l], sgu_norm_g[l], sgu_w[l], sgu_b[l], pool_w[l], pool_b[l],
                   pool_scale[l], conv_w[l], conv_b[l], lru_wa[l], lru_ba[l], lru_wx[l], lru_bx[l],
                   lru_lambda[l], branch_norm_g[l], w_out[l], final_g,
                   is_last=(l == DEPTH - 1), name=f"hybrid_layer{l}")
    return x
```

```python
import functools

import jax
import jax.numpy as jnp
from jax import lax
from jax.experimental import pallas as pl
from jax.experimental.pallas import tpu as pltpu

D_MODEL = 1024
DEPTH = 2
CHUNK = 64
EPS = 1e-6
A_WIDTH = 256
A_HEADS = 4
A_HEAD_DIM = 64
SGU_BLOCK = 128
B_WIDTH = 256
POOL_WINDOWS = (2, 4, 8, 16)
B_GROUP_DIM = 64
C_WIDTH = 512
C_HEADS = 8
C_HEAD_DIM = 64
CONV_WIDTH = 4
LRU_C = 8.0
D_IN = 3 * A_WIDTH + 2 * B_WIDTH + 2 * C_WIDTH

SUBLANES = 8
LANES = 128
TOK = 512
POOL_HALO = 16
CONV_HALO = 8
C_HALF = C_WIDTH // 2
VMEM_LIMIT_BYTES = 48 * 1024 * 1024

_BF16 = jnp.bfloat16
_F32 = jnp.float32


def _rmsnorm(x, g):
    ms = jnp.mean(x * x, axis=-1, keepdims=True)
    return x * lax.rsqrt(ms + EPS) * g


def _sigmoid(z):
    return 0.5 * jnp.tanh(0.5 * z) + 0.5


def _silu(z):
    h = 0.5 * z
    return h * jnp.tanh(h) + h


def _scan_half(a, b, h0, h_ref, col):
    groups = TOK // SUBLANES
    a3 = a.reshape(groups, SUBLANES, C_HALF)
    b3 = b.reshape(groups, SUBLANES, C_HALF)
    sub = lax.broadcasted_iota(jnp.int32, (1, SUBLANES, C_HALF), 1)
    k = 1
    while k < SUBLANES:
        m = sub >= k
        a_sh = jnp.where(m, pltpu.roll(a3, k, axis=1), 1.0)
        b_sh = jnp.where(m, pltpu.roll(b3, k, axis=1), 0.0)
        b3 = a3 * b_sh + b3
        a3 = a3 * a_sh
        k *= 2
    hb = h0
    for g in range(groups):
        hv = a3[g] * hb + b3[g]
        h_ref[g * SUBLANES:(g + 1) * SUBLANES, col:col + C_HALF] = hv
        hb = jnp.broadcast_to(hv[SUBLANES - 1:SUBLANES, :], (SUBLANES, C_HALF))
    return hb


def _layer_kernel(x_ref, norm_g_ref, w_in_ref, sgu_g_ref, sgu_w_ref, sgu_bias_ref,
                  pool_w_ref, pool_b_ref, pool_scale_ref, conv_w_ref, conv_b_ref,
                  gate_w_ref, b_a_ref, b_x_ref, lam_ref, bng_ref, w_out_ref, final_g_ref,
                  o_ref,
                  xb_ext, s2_ext, s4_ext, s8_ext, cx_ext, h_carry, h_buf, y_buf,
                  *, is_last):
    c = pl.program_id(1)

    @pl.when(c == 0)
    def _reset():
        xb_ext[0:POOL_HALO, :] = jnp.zeros((POOL_HALO, B_WIDTH), _F32)
        s2_ext[0:POOL_HALO, :] = jnp.zeros((POOL_HALO, B_WIDTH), _F32)
        s4_ext[0:POOL_HALO, :] = jnp.zeros((POOL_HALO, LANES), _F32)
        s8_ext[0:POOL_HALO, :] = jnp.zeros((POOL_HALO, LANES), _F32)
        cx_ext[0:CONV_HALO, :] = jnp.zeros((CONV_HALO, C_WIDTH), _F32)
        h_carry[...] = jnp.zeros((SUBLANES, C_WIDTH), _F32)

    x = x_ref[...]
    hn = _rmsnorm(x, norm_g_ref[...]).astype(_BF16)
    bng = bng_ref[...]

    off = 0
    proj_a = jnp.dot(hn, w_in_ref[:, off:off + 3 * A_WIDTH], preferred_element_type=_F32)
    a_u = proj_a[:, 0:A_WIDTH]
    a_v = proj_a[:, A_WIDTH:2 * A_WIDTH]
    a_z = proj_a[:, 2 * A_WIDTH:3 * A_WIDTH]
    v = _rmsnorm(a_v, sgu_g_ref[...]).astype(_BF16)
    ci = lax.broadcasted_iota(jnp.int32, (SGU_BLOCK, SGU_BLOCK), 0) // CHUNK
    cj = lax.broadcasted_iota(jnp.int32, (SGU_BLOCK, SGU_BLOCK), 1) // CHUNK
    mask = (cj <= ci).astype(_F32)
    wm = jnp.concatenate([(sgu_w_ref[h] * mask).astype(_BF16) for h in range(A_HEADS)], axis=0)
    lane_a = lax.broadcasted_iota(jnp.int32, (1, A_WIDTH), 1)
    ys = []
    for n in range(TOK // SGU_BLOCK):
        r = jnp.dot(wm, v[n * SGU_BLOCK:(n + 1) * SGU_BLOCK, :], preferred_element_type=_F32)
        yn = r[(A_HEADS - 1) * SGU_BLOCK:, :]
        for h in range(A_HEADS - 2, -1, -1):
            yn = jnp.where(lane_a < (h + 1) * A_HEAD_DIM, r[h * SGU_BLOCK:(h + 1) * SGU_BLOCK, :], yn)
        ys.append(yn + sgu_bias_ref[...])
    ya = a_u * jnp.concatenate(ys, axis=0)
    ya = _rmsnorm(ya, bng[:, 0:A_WIDTH]) * _silu(a_z)
    y_buf[:, 0:A_WIDTH] = ya.astype(_BF16)

    off += 3 * A_WIDTH
    proj_b = jnp.dot(hn, w_in_ref[:, off:off + 2 * B_WIDTH], preferred_element_type=_F32)
    bx = proj_b[:, 0:B_WIDTH]
    b_z = proj_b[:, B_WIDTH:2 * B_WIDTH]
    xb_ext[POOL_HALO:POOL_HALO + TOK, :] = bx
    s2 = bx + xb_ext[pl.ds(POOL_HALO - 1, TOK), :]
    s2_ext[POOL_HALO:POOL_HALO + TOK, :] = s2
    s4 = s2 + s2_ext[pl.ds(POOL_HALO - 2, TOK), :]
    s4_hi = s4[:, LANES:]
    s4_ext[POOL_HALO:POOL_HALO + TOK, :] = s4_hi
    s8 = s4_hi + s4_ext[pl.ds(POOL_HALO - 4, TOK), :]
    s8_ext[POOL_HALO:POOL_HALO + TOK, :] = s8
    s16 = s8 + s8_ext[pl.ds(POOL_HALO - 8, TOK), :]
    lane = lax.broadcasted_iota(jnp.int32, (1, LANES), 1)
    wsum = jnp.concatenate([jnp.where(lane < B_GROUP_DIM, s2[:, :LANES], s4[:, :LANES]),
                            jnp.where(lane < B_GROUP_DIM, s8, s16)], axis=1)
    lane_b = lax.broadcasted_iota(jnp.int32, (1, B_WIDTH), 1) // B_GROUP_DIM
    win = jnp.where(lane_b == 0, POOL_WINDOWS[0],
                    jnp.where(lane_b == 1, POOL_WINDOWS[1],
                              jnp.where(lane_b == 2, POOL_WINDOWS[2], POOL_WINDOWS[3])))
    inv_win = 1.0 / win.astype(_F32)
    t_head = c * TOK + lax.broadcasted_iota(jnp.int32, (POOL_HALO, B_WIDTH), 0)
    cnt_head = jnp.minimum(t_head + 1, win).astype(_F32)
    mean = jnp.concatenate([wsum[0:POOL_HALO] / cnt_head, wsum[POOL_HALO:] * inv_win], axis=0)
    p = (mean - bx).astype(_BF16)
    yb = jnp.dot(p, pool_w_ref[...], preferred_element_type=_F32) + pool_b_ref[...]
    yb = yb * pool_scale_ref[...]
    yb = _rmsnorm(yb, bng[:, A_WIDTH:A_WIDTH + B_WIDTH]) * _silu(b_z)
    y_buf[:, A_WIDTH:A_WIDTH + B_WIDTH] = yb.astype(_BF16)
    xb_ext[0:POOL_HALO, :] = xb_ext[TOK:TOK + POOL_HALO, :]
    s2_ext[0:POOL_HALO, :] = s2_ext[TOK:TOK + POOL_HALO, :]
    s4_ext[0:POOL_HALO, :] = s4_ext[TOK:TOK + POOL_HALO, :]
    s8_ext[0:POOL_HALO, :] = s8_ext[TOK:TOK + POOL_HALO, :]

    off += 2 * B_WIDTH
    proj_c = jnp.dot(hn, w_in_ref[:, off:off + 2 * C_WIDTH], preferred_element_type=_F32)
    cx = proj_c[:, 0:C_WIDTH]
    c_z = proj_c[:, C_WIDTH:2 * C_WIDTH]
    cx_ext[CONV_HALO:CONV_HALO + TOK, :] = cx
    cw = conv_w_ref[...]
    conv = conv_b_ref[...] + cx * cw[CONV_WIDTH - 1:CONV_WIDTH, :]
    for k in range(CONV_WIDTH - 1):
        shift = CONV_WIDTH - 1 - k
        conv = conv + cx_ext[pl.ds(CONV_HALO - shift, TOK), :] * cw[k:k + 1, :]
    cx_ext[0:CONV_HALO, :] = cx_ext[TOK:TOK + CONV_HALO, :]
    conv_bf = conv.astype(_BF16)
    neg_c_softplus = -LRU_C * jax.nn.softplus(-lam_ref[...])
    hc = h_carry[...]
    ssq = jnp.zeros((TOK, 1), _F32)
    for d in range(2):
        lo = d * C_HALF
        gates = jnp.dot(conv_bf[:, lo:lo + C_HALF], gate_w_ref[d], preferred_element_type=_F32)
        r = _sigmoid(gates[:, 0:C_HALF] + b_a_ref[:, lo:lo + C_HALF])
        i = _sigmoid(gates[:, C_HALF:] + b_x_ref[:, lo:lo + C_HALF])
        log_a = r * neg_c_softplus[:, lo:lo + C_HALF]
        a = jnp.exp(log_a)
        msq = -jnp.tanh(log_a) * (a * a + 1.0)
        mult = jnp.where(msq > 0.0, msq * lax.rsqrt(msq), 0.0)
        bterm = mult * (i * conv[:, lo:lo + C_HALF])
        hb = _scan_half(a, bterm, hc[:, lo:lo + C_HALF], h_buf, lo)
        h_carry[:, lo:lo + C_HALF] = hb
    hs = h_buf[...]
    yc = _rmsnorm(hs, bng[:, A_WIDTH + B_WIDTH:]) * _silu(c_z)
    y_buf[:, A_WIDTH + B_WIDTH:] = yc.astype(_BF16)

    out = x + jnp.dot(y_buf[...], w_out_ref[...], preferred_element_type=_F32)
    if is_last:
        out = _rmsnorm(out, final_g_ref[...])
    o_ref[...] = out


def _full(shape):
    return pl.BlockSpec(shape, lambda b, c: (0,) * len(shape))


def _block_diag(w):
    n, d, _ = w.shape
    eye = jnp.eye(n, dtype=w.dtype)
    return (eye[:, None, :, None] * w[:, :, None, :]).reshape(n * d, n * d)


def _layer(x, norm_g, w_in, sgu_g, sgu_w, sgu_b, pool_w, pool_b, pool_scale, conv_w, conv_b,
           lru_wa, lru_ba, lru_wx, lru_bx, lam, bng, w_out, final_g, *, is_last, name):
    bsz, seq, _ = x.shape
    assert seq % TOK == 0 and TOK % SGU_BLOCK == 0
    row = lambda a: a.reshape(1, -1).astype(_F32)
    heads_per_half = C_HEADS // 2
    gate_w = jnp.stack([
        jnp.concatenate([_block_diag(lru_wa[d * heads_per_half:(d + 1) * heads_per_half]),
                         _block_diag(lru_wx[d * heads_per_half:(d + 1) * heads_per_half])], axis=1)
        for d in range(2)]).astype(_BF16)
    sgu_bias = jnp.repeat(sgu_b.T, A_HEAD_DIM, axis=1).astype(_F32)
    sgu_bias = sgu_bias
    operands = (
        x, row(norm_g), w_in.astype(_BF16), row(sgu_g), sgu_w.astype(_F32), sgu_bias,
        _block_diag(pool_w).astype(_BF16), row(pool_b), row(pool_scale),
        conv_w.astype(_F32), row(conv_b), gate_w, row(lru_ba), row(lru_bx), row(lam),
        row(bng), w_out.astype(_BF16), row(final_g),
    )
    in_specs = [pl.BlockSpec((None, TOK, D_MODEL), lambda b, c: (b, c, 0))]
    in_specs += [_full(op.shape) for op in operands[1:]]
    return pl.pallas_call(
        functools.partial(_layer_kernel, is_last=is_last),
        out_shape=jax.ShapeDtypeStruct(x.shape, x.dtype),
        grid=(bsz, seq // TOK),
        in_specs=in_specs,
        out_specs=pl.BlockSpec((None, TOK, D_MODEL), lambda b, c: (b, c, 0)),
        scratch_shapes=[
            pltpu.VMEM((TOK + POOL_HALO, B_WIDTH), _F32),
            pltpu.VMEM((TOK + POOL_HALO, B_WIDTH), _F32),
            pltpu.VMEM((TOK + POOL_HALO, LANES), _F32),
            pltpu.VMEM((TOK + POOL_HALO, LANES), _F32),
            pltpu.VMEM((TOK + CONV_HALO, C_WIDTH), _F32),
            pltpu.VMEM((SUBLANES, C_WIDTH), _F32),
            pltpu.VMEM((TOK, C_WIDTH), _F32),
            pltpu.VMEM((TOK, D_MODEL), _BF16),
        ],
        compiler_params=pltpu.CompilerParams(
            dimension_semantics=("arbitrary", "arbitrary"),
            vmem_limit_bytes=VMEM_LIMIT_BYTES,
        ),
        name=name,
    )(*operands)


def kernel(x, norm_g, w_in, sgu_norm_g, sgu_w, sgu_b, pool_w, pool_b, pool_scale, conv_w, conv_b,
           lru_wa, lru_ba, lru_wx, lru_bx, lru_lambda, branch_norm_g, w_out, final_g):
    for l in range(DEPTH):
        x = _layer(x, norm_g[l], w_in[l], sgu_norm_g[l], sgu_w[l], sgu_b[l], pool_w[l], pool_b[l],
                   pool_scale[l], conv_w[l], conv_b[l], lru_wa[l], lru_ba[l], lru_wx[l], lru_bx[l],
                   lru_lambda[l], branch_norm_g[l], w_out[l], final_g,
                   is_last=(l == DEPTH - 1), name=f"hybrid_layer{l}")
    return x
```

```python
import functools

import jax
import jax.numpy as jnp
from jax import lax
from jax.experimental import pallas as pl
from jax.experimental.pallas import tpu as pltpu

D_MODEL = 1024
DEPTH = 2
CHUNK = 64
EPS = 1e-6
A_WIDTH = 256
A_HEADS = 4
A_HEAD_DIM = 64
SGU_BLOCK = 128
B_WIDTH = 256
POOL_WINDOWS = (2, 4, 8, 16)
B_GROUP_DIM = 64
C_WIDTH = 512
C_HEADS = 8
C_HEAD_DIM = 64
CONV_WIDTH = 4
LRU_C = 8.0
D_IN = 3 * A_WIDTH + 2 * B_WIDTH + 2 * C_WIDTH
OFF_B = 3 * A_WIDTH
OFF_C = OFF_B + 2 * B_WIDTH

SUBLANES = 8
LANES = 128
MXU_COLS = 256
TOK = 256
PAIR = 2
LAG_STEPS = 1
POOL_HALO = 16
CONV_HALO = 8
C_HALF = C_WIDTH // 2
VMEM_LIMIT_BYTES = 56 * 1024 * 1024

_BF16 = jnp.bfloat16
_F32 = jnp.float32


def _rmsnorm(x, g):
    ms = jnp.mean(x * x, axis=-1, keepdims=True)
    return x * lax.rsqrt(ms + EPS) * g


def _sigmoid(z):
    return 0.5 * jnp.tanh(0.5 * z) + 0.5


def _silu(z):
    h = 0.5 * z
    return h * jnp.tanh(h) + h


def _scan_level1(a, b):
    groups = TOK // SUBLANES
    a3 = a.reshape(groups, SUBLANES, C_HALF)
    b3 = b.reshape(groups, SUBLANES, C_HALF)
    sub = lax.broadcasted_iota(jnp.int32, (1, SUBLANES, C_HALF), 1)
    k = 1
    while k < SUBLANES:
        m = sub >= k
        a_sh = jnp.where(m, pltpu.roll(a3, k, axis=1), 1.0)
        b_sh = jnp.where(m, pltpu.roll(b3, k, axis=1), 0.0)
        b3 = a3 * b_sh + b3
        a3 = a3 * a_sh
        k *= 2
    return a3, b3


def _scan_level2(a3, b3, h0, h_ref, col):
    hb = h0
    for g in range(TOK // SUBLANES):
        hv = a3[g] * hb + b3[g]
        h_ref[g * SUBLANES:(g + 1) * SUBLANES, col:col + C_HALF] = hv
        hb = jnp.broadcast_to(hv[SUBLANES - 1:SUBLANES, :], (SUBLANES, C_HALF))
    return hb


def _mix_pieces(proj_ref, y_ref, cm, p, s):
    first = cm == 0
    bng = p["bng"][...]

    for name in ("xb_ext", "s2_ext", "s4_ext", "s8_ext"):
        ref = s[name]
        ref[0:POOL_HALO, :] = jnp.where(first, 0.0, ref[0:POOL_HALO, :])
    s["cx_ext"][0:CONV_HALO, :] = jnp.where(first, 0.0, s["cx_ext"][0:CONV_HALO, :])
    hc = jnp.where(first, 0.0, s["h_carry"][...])

    v = _rmsnorm(proj_ref[:, A_WIDTH:2 * A_WIDTH], p["sgu_g"][...]).astype(_BF16)
    ci = lax.broadcasted_iota(jnp.int32, (SGU_BLOCK, SGU_BLOCK), 0) // CHUNK
    cj = lax.broadcasted_iota(jnp.int32, (SGU_BLOCK, SGU_BLOCK), 1) // CHUNK
    mask = (cj <= ci).astype(_F32)
    wm = jnp.concatenate([(p["sgu_w"][h] * mask).astype(_BF16) for h in range(A_HEADS)], axis=0)
    lane_a = lax.broadcasted_iota(jnp.int32, (1, A_WIDTH), 1)
    ys = []
    for n in range(TOK // SGU_BLOCK):
        r = jnp.dot(wm, v[n * SGU_BLOCK:(n + 1) * SGU_BLOCK, :], preferred_element_type=_F32)
        yn = r[(A_HEADS - 1) * SGU_BLOCK:, :]
        for h in range(A_HEADS - 2, -1, -1):
            yn = jnp.where(lane_a < (h + 1) * A_HEAD_DIM, r[h * SGU_BLOCK:(h + 1) * SGU_BLOCK, :], yn)
        ys.append(yn + p["sgu_bias"][...])
    ya = proj_ref[:, 0:A_WIDTH] * jnp.concatenate(ys, axis=0)
    yield
    ya = _rmsnorm(ya, bng[:, 0:A_WIDTH]) * _silu(proj_ref[:, 2 * A_WIDTH:3 * A_WIDTH])
    y_ref[:, 0:A_WIDTH] = ya.astype(_BF16)
    yield

    xb_ext, s2_ext, s4_ext, s8_ext = s["xb_ext"], s["s2_ext"], s["s4_ext"], s["s8_ext"]
    bx = proj_ref[:, OFF_B:OFF_B + B_WIDTH]
    xb_ext[POOL_HALO:POOL_HALO + TOK, :] = bx
    s2 = bx + xb_ext[pl.ds(POOL_HALO - 1, TOK), :]
    s2_ext[POOL_HALO:POOL_HALO + TOK, :] = s2
    s4 = s2 + s2_ext[pl.ds(POOL_HALO - 2, TOK), :]
    s4_hi = s4[:, LANES:]
    s4_ext[POOL_HALO:POOL_HALO + TOK, :] = s4_hi
    s8 = s4_hi + s4_ext[pl.ds(POOL_HALO - 4, TOK), :]
    s8_ext[POOL_HALO:POOL_HALO + TOK, :] = s8
    s16 = s8 + s8_ext[pl.ds(POOL_HALO - 8, TOK), :]
    lane = lax.broadcasted_iota(jnp.int32, (1, LANES), 1)
    wsum = jnp.concatenate([jnp.where(lane < B_GROUP_DIM, s2[:, :LANES], s4[:, :LANES]),
                            jnp.where(lane < B_GROUP_DIM, s8, s16)], axis=1)
    lane_b = lax.broadcasted_iota(jnp.int32, (1, B_WIDTH), 1) // B_GROUP_DIM
    win = jnp.where(lane_b == 0, POOL_WINDOWS[0],
                    jnp.where(lane_b == 1, POOL_WINDOWS[1],
                              jnp.where(lane_b == 2, POOL_WINDOWS[2], POOL_WINDOWS[3])))
    inv_win = 1.0 / win.astype(_F32)
    t_head = cm * TOK + lax.broadcasted_iota(jnp.int32, (POOL_HALO, B_WIDTH), 0)
    cnt_head = jnp.minimum(t_head + 1, win).astype(_F32)
    mean = jnp.concatenate([wsum[0:POOL_HALO] / cnt_head, wsum[POOL_HALO:] * inv_win], axis=0)
    pooled = (mean - bx).astype(_BF16)
    for ref in (xb_ext, s2_ext, s4_ext, s8_ext):
        ref[0:POOL_HALO, :] = ref[TOK:TOK + POOL_HALO, :]
    yield
    yb = jnp.dot(pooled, p["pool_w"][...], preferred_element_type=_F32) + p["pool_b"][...]
    yb = yb * p["pool_scale"][...]
    yb = _rmsnorm(yb, bng[:, A_WIDTH:A_WIDTH + B_WIDTH])
    yb = yb * _silu(proj_ref[:, OFF_B + B_WIDTH:OFF_B + 2 * B_WIDTH])
    y_ref[:, A_WIDTH:A_WIDTH + B_WIDTH] = yb.astype(_BF16)
    yield

    cx_ext = s["cx_ext"]
    cw = p["conv_w"][...]
    neg_c_softplus = -LRU_C * jax.nn.softplus(-p["lam"][...])
    h_buf = s["h_buf"]
    cx_ext[CONV_HALO:CONV_HALO + TOK, :] = proj_ref[:, OFF_C:OFF_C + C_WIDTH]

    def conv_half(lo):
        acc = p["conv_b"][:, lo:lo + C_HALF] + (cx_ext[CONV_HALO:CONV_HALO + TOK, lo:lo + C_HALF]
                                                * cw[CONV_WIDTH - 1:CONV_WIDTH, lo:lo + C_HALF])
        for k in range(CONV_WIDTH - 1):
            shift = CONV_WIDTH - 1 - k
            acc = acc + cx_ext[pl.ds(CONV_HALO - shift, TOK), lo:lo + C_HALF] * cw[k:k + 1, lo:lo + C_HALF]
        return acc

    conv = conv_half(0)
    yield
    for d in range(2):
        lo = d * C_HALF
        gates = jnp.dot(conv.astype(_BF16), p["gate_w"][d], preferred_element_type=_F32)
        r = _sigmoid(gates[:, 0:C_HALF] + p["b_a"][:, lo:lo + C_HALF])
        i = _sigmoid(gates[:, C_HALF:] + p["b_x"][:, lo:lo + C_HALF])
        log_a = r * neg_c_softplus[:, lo:lo + C_HALF]
        a = jnp.exp(log_a)
        msq = -jnp.tanh(log_a) * (a * a + 1.0)
        mult = jnp.where(msq > 0.0, msq * lax.rsqrt(msq), 0.0)
        bterm = mult * (i * conv)
        yield
        a3, b3 = _scan_level1(a, bterm)
        yield
        hb = _scan_level2(a3, b3, hc[:, lo:lo + C_HALF], h_buf, lo)
        s["h_carry"][:, lo:lo + C_HALF] = hb
        if d == 0:
            conv = conv_half(C_HALF)
            cx_ext[0:CONV_HALO, :] = cx_ext[TOK:TOK + CONV_HALO, :]
            yield
    yc = _rmsnorm(h_buf[...], bng[:, A_WIDTH + B_WIDTH:])
    yc = yc * _silu(proj_ref[:, OFF_C + C_WIDTH:OFF_C + 2 * C_WIDTH])
    y_ref[:, A_WIDTH + B_WIDTH:] = yc.astype(_BF16)
    yield


def _matmul_tiles(lhs_ref, w_ref, n_cols, emit):
    def tile(c0):
        cols = slice(c0, c0 + MXU_COLS)
        emit(cols, jnp.dot(lhs_ref[...], w_ref[:, cols], preferred_element_type=_F32))
    return [functools.partial(tile, c0) for c0 in range(0, n_cols, MXU_COLS)]


_PARAM_NAMES = ("norm_g", "w_in", "sgu_g", "sgu_w", "sgu_bias", "pool_w", "pool_b", "pool_scale",
                "conv_w", "conv_b", "gate_w", "b_a", "b_x", "lam", "bng", "w_out", "final_g")
_STATE_NAMES = ("xb_ext", "s2_ext", "s4_ext", "s8_ext", "cx_ext", "h_carry", "h_buf")


def _layer_kernel(*refs, is_last, chunks_per_seq):
    x_ref, xlag_ref = refs[0], refs[1]
    p = dict(zip(_PARAM_NAMES, refs[2:2 + len(_PARAM_NAMES)]))
    o_ref = refs[2 + len(_PARAM_NAMES)]
    scratch = refs[3 + len(_PARAM_NAMES):]
    proj_bufs = scratch[0:PAIR]
    y_bufs = scratch[PAIR:2 * PAIR]
    hn_buf = scratch[2 * PAIR]
    s = dict(zip(_STATE_NAMES, scratch[2 * PAIR + 1:]))
    k = pl.program_id(0)

    @pl.when(k == 0)
    def _init():
        for ref in scratch:
            ref[...] = jnp.zeros(ref.shape, ref.dtype)

    for j in range(PAIR):
        rows = slice(j * TOK, (j + 1) * TOK)
        proj_ref = proj_bufs[j]

        def emit_out(cols, t, rows=rows):
            o_ref[rows, cols] = xlag_ref[rows, cols] + t

        mxu_work = _matmul_tiles(y_bufs[j], p["w_out"], D_MODEL, emit_out)

        def emit_proj(cols, t, proj_ref=proj_ref):
            proj_ref[:, cols] = t

        mxu_work += _matmul_tiles(hn_buf, p["w_in"], D_IN, emit_proj)

        cm = lax.rem(PAIR * k + j + chunks_per_seq - 1, chunks_per_seq)
        mix = _mix_pieces(proj_bufs[1 - j], y_bufs[1 - j], cm, p, s)

        n_out_tiles = D_MODEL // MXU_COLS
        for t, thunk in enumerate(mxu_work):
            if t == 0:
                hn_buf[...] = _rmsnorm(x_ref[rows, :], p["norm_g"][...]).astype(_BF16)
            else:
                next(mix, None)
            thunk()
            if is_last and t == n_out_tiles - 1:
                o_ref[rows, :] = _rmsnorm(o_ref[rows, :], p["final_g"][...])
        for _ in mix:
            pass


def _block_diag(w):
    n, d, _ = w.shape
    eye = jnp.eye(n, dtype=w.dtype)
    return (eye[:, None, :, None] * w[:, :, None, :]).reshape(n * d, n * d)


def _layer(x, norm_g, w_in, sgu_g, sgu_w, sgu_b, pool_w, pool_b, pool_scale, conv_w, conv_b,
           lru_wa, lru_ba, lru_wx, lru_bx, lam, bng, w_out, final_g, *, is_last, name):
    bsz, seq, _ = x.shape
    step_rows = PAIR * TOK
    assert seq % step_rows == 0 and TOK % SGU_BLOCK == 0 and PAIR == 2
    steps_per_seq = seq // step_rows
    n_real = bsz * steps_per_seq
    row = lambda a: a.reshape(1, -1).astype(_F32)
    heads_per_half = C_HEADS // 2
    gate_w = jnp.stack([
        jnp.concatenate([_block_diag(lru_wa[d * heads_per_half:(d + 1) * heads_per_half]),
                         _block_diag(lru_wx[d * heads_per_half:(d + 1) * heads_per_half])], axis=1)
        for d in range(2)]).astype(_BF16)
    sgu_bias = jnp.repeat(sgu_b.T, A_HEAD_DIM, axis=1).astype(_F32)
    params = dict(
        norm_g=row(norm_g), w_in=w_in.astype(_BF16), sgu_g=row(sgu_g), sgu_w=sgu_w.astype(_F32),
        sgu_bias=sgu_bias, pool_w=_block_diag(pool_w).astype(_BF16), pool_b=row(pool_b),
        pool_scale=row(pool_scale), conv_w=conv_w.astype(_F32), conv_b=row(conv_b), gate_w=gate_w,
        b_a=row(lru_ba), b_x=row(lru_bx), lam=row(lam), bng=row(bng), w_out=w_out.astype(_BF16),
        final_g=row(final_g))
    operands = (x, x) + tuple(params[n] for n in _PARAM_NAMES)

    def x_map(k):
        kk = jnp.minimum(k, n_real - 1)
        return (kk // steps_per_seq, kk % steps_per_seq, 0)

    def lag_map(k):
        kl = jnp.maximum(k - LAG_STEPS, 0)
        return (kl // steps_per_seq, kl % steps_per_seq, 0)

    def full(shape):
        return pl.BlockSpec(shape, lambda k: (0,) * len(shape))

    in_specs = [pl.BlockSpec((None, step_rows, D_MODEL), x_map),
                pl.BlockSpec((None, step_rows, D_MODEL), lag_map)]
    in_specs += [full(op.shape) for op in operands[2:]]
    scratch_shapes = (
        [pltpu.VMEM((TOK, D_IN), _F32) for _ in range(PAIR)]
        + [pltpu.VMEM((TOK, D_MODEL), _BF16) for _ in range(PAIR)]
        + [pltpu.VMEM((TOK, D_MODEL), _BF16),
           pltpu.VMEM((TOK + POOL_HALO, B_WIDTH), _F32),
           pltpu.VMEM((TOK + POOL_HALO, B_WIDTH), _F32),
           pltpu.VMEM((TOK + POOL_HALO, LANES), _F32),
           pltpu.VMEM((TOK + POOL_HALO, LANES), _F32),
           pltpu.VMEM((TOK + CONV_HALO, C_WIDTH), _F32),
           pltpu.VMEM((SUBLANES, C_WIDTH), _F32),
           pltpu.VMEM((TOK, C_WIDTH), _F32)])
    return pl.pallas_call(
        functools.partial(_layer_kernel, is_last=is_last, chunks_per_seq=seq // TOK),
        out_shape=jax.ShapeDtypeStruct(x.shape, x.dtype),
        grid=(n_real + LAG_STEPS,),
        in_specs=in_specs,
        out_specs=pl.BlockSpec((None, step_rows, D_MODEL), lag_map),
        scratch_shapes=scratch_shapes,
        compiler_params=pltpu.CompilerParams(
            dimension_semantics=("arbitrary",),
            vmem_limit_bytes=VMEM_LIMIT_BYTES,
        ),
        name=name,
    )(*operands)


def kernel(x, norm_g, w_in, sgu_norm_g, sgu_w, sgu_b, pool_w, pool_b, pool_scale, conv_w, conv_b,
           lru_wa, lru_ba, lru_wx, lru_bx, lru_lambda, branch_norm_g, w_out, final_g):
    for l in range(DEPTH):
        x = _layer(x, norm_g[l], w_in[l], sgu_norm_g[l], sgu_w[l], sgu_b[l], pool_w[l], pool_b[l],
                   pool_scale[l], conv_w[l], conv_b[l], lru_wa[l], lru_ba[l], lru_wx[l], lru_bx[l],
                   lru_lambda[l], branch_norm_g[l], w_out[l], final_g,
                   is_last=(l == DEPTH - 1), name=f"hybrid_layer{l}")
    return x
```

```python
import functools

import jax
import jax.numpy as jnp
from jax import lax
from jax.experimental import pallas as pl
from jax.experimental.pallas import tpu as pltpu

D_MODEL = 1024
DEPTH = 2
CHUNK = 64
EPS = 1e-6
A_WIDTH = 256
A_HEADS = 4
A_HEAD_DIM = 64
SGU_BLOCK = 128
B_WIDTH = 256
POOL_WINDOWS = (2, 4, 8, 16)
B_GROUP_DIM = 64
C_WIDTH = 512
C_HEADS = 8
C_HEAD_DIM = 64
CONV_WIDTH = 4
LRU_C = 8.0
D_IN = 3 * A_WIDTH + 2 * B_WIDTH + 2 * C_WIDTH
OFF_B = 3 * A_WIDTH
OFF_C = OFF_B + 2 * B_WIDTH

SUBLANES = 8
LANES = 128
MXU_COLS = 256
TOK = 256
PAIR = 2
LAG_STEPS = 1
POOL_HALO = 16
CONV_HALO = 8
C_HALF = C_WIDTH // 2
SEGS = SUBLANES
SEG_ROWS = TOK // SEGS
SEG_PITCH = SEG_ROWS + CONV_HALO
VMEM_LIMIT_BYTES = 56 * 1024 * 1024

_BF16 = jnp.bfloat16
_F32 = jnp.float32


def _rmsnorm(x, g):
    ms = jnp.mean(x * x, axis=-1, keepdims=True)
    return x * lax.rsqrt(ms + EPS) * g


def _sigmoid(z):
    return 0.5 * jnp.tanh(0.5 * z) + 0.5


def _silu(z):
    h = 0.5 * z
    return h * jnp.tanh(h) + h


def _seg_store(seg_ref, x_ref, col0, width, halo):
    for slab in range(width // LANES):
        lanes = slice(col0 + slab * LANES, col0 + (slab + 1) * LANES)
        for g in range(SEGS):
            base = g * SEG_PITCH
            if g == 0:
                before = halo[:, slab * LANES:(slab + 1) * LANES]
            else:
                before = x_ref[g * SEG_ROWS - CONV_HALO:g * SEG_ROWS, lanes]
            seg_ref[slab, base:base + CONV_HALO, :] = before
            seg_ref[slab, base + CONV_HALO:base + SEG_PITCH, :] = x_ref[g * SEG_ROWS:(g + 1) * SEG_ROWS, lanes]


def _seg_row(seg_ref, r, slabs):
    return jnp.concatenate([seg_ref[slab, pl.ds(r, SEGS, stride=SEG_PITCH), :] for slab in slabs], axis=1)


def _segment_scan(a, b, h0, seg_ref, slabs):
    width = a.shape[-1]
    a3 = a.reshape(SEG_ROWS, SEGS, width)
    b3 = b.reshape(SEG_ROWS, SEGS, width)
    acum, hloc = [a3[0]], [b3[0]]
    for j in range(1, SEG_ROWS):
        acum.append(acum[-1] * a3[j])
        hloc.append(a3[j] * hloc[-1] + b3[j])
    end_a, end_h = acum[-1], hloc[-1]
    sub = lax.broadcasted_iota(jnp.int32, (SEGS, width), 0)
    carry = h0
    for _ in range(SEGS - 1):
        carry = jnp.where(sub == 0, h0, pltpu.roll(end_h + end_a * carry, 1, axis=0))
    for j in range(SEG_ROWS):
        hj = hloc[j] + acum[j] * carry
        for n, slab in enumerate(slabs):
            seg_ref[slab, pl.ds(j, SEGS, stride=SEG_PITCH), :] = hj[:, n * LANES:(n + 1) * LANES]
    last = end_h + end_a * carry
    return jnp.broadcast_to(last[SEGS - 1:SEGS, :], (SEGS, width))


def _seg_load_time_order(seg_ref, slabs):
    return jnp.concatenate(
        [jnp.concatenate([seg_ref[slab, g * SEG_PITCH:g * SEG_PITCH + SEG_ROWS, :] for g in range(SEGS)], axis=0)
         for slab in slabs], axis=1)


def _mix_pieces(proj_ref, y_ref, cm, p, s):
    first = cm == 0
    bng = p["bng"][...]

    for name in ("xb_ext", "s2_ext", "s4_ext", "s8_ext"):
        ref = s[name]
        ref[0:POOL_HALO, :] = jnp.where(first, 0.0, ref[0:POOL_HALO, :])
    cx_halo = jnp.where(first, 0.0, s["cx_halo"][...])
    hc = jnp.where(first, 0.0, s["h_carry"][...])

    v = _rmsnorm(proj_ref[:, A_WIDTH:2 * A_WIDTH], p["sgu_g"][...]).astype(_BF16)
    ci = lax.broadcasted_iota(jnp.int32, (SGU_BLOCK, SGU_BLOCK), 0) // CHUNK
    cj = lax.broadcasted_iota(jnp.int32, (SGU_BLOCK, SGU_BLOCK), 1) // CHUNK
    mask = (cj <= ci).astype(_F32)
    wm = jnp.concatenate([(p["sgu_w"][h] * mask).astype(_BF16) for h in range(A_HEADS)], axis=0)
    lane_a = lax.broadcasted_iota(jnp.int32, (1, A_WIDTH), 1)
    ys = []
    for n in range(TOK // SGU_BLOCK):
        r = jnp.dot(wm, v[n * SGU_BLOCK:(n + 1) * SGU_BLOCK, :], preferred_element_type=_F32)
        yn = r[(A_HEADS - 1) * SGU_BLOCK:, :]
        for h in range(A_HEADS - 2, -1, -1):
            yn = jnp.where(lane_a < (h + 1) * A_HEAD_DIM, r[h * SGU_BLOCK:(h + 1) * SGU_BLOCK, :], yn)
        ys.append(yn + p["sgu_bias"][...])
    ya = proj_ref[:, 0:A_WIDTH] * jnp.concatenate(ys, axis=0)
    yield
    ya = _rmsnorm(ya, bng[:, 0:A_WIDTH]) * _silu(proj_ref[:, 2 * A_WIDTH:3 * A_WIDTH])
    y_ref[:, 0:A_WIDTH] = ya.astype(_BF16)
    yield

    xb_ext, s2_ext, s4_ext, s8_ext = s["xb_ext"], s["s2_ext"], s["s4_ext"], s["s8_ext"]
    bx = proj_ref[:, OFF_B:OFF_B + B_WIDTH]
    xb_ext[POOL_HALO:POOL_HALO + TOK, :] = bx
    s2 = bx + xb_ext[pl.ds(POOL_HALO - 1, TOK), :]
    s2_ext[POOL_HALO:POOL_HALO + TOK, :] = s2
    s4 = s2 + s2_ext[pl.ds(POOL_HALO - 2, TOK), :]
    s4_hi = s4[:, LANES:]
    s4_ext[POOL_HALO:POOL_HALO + TOK, :] = s4_hi
    s8 = s4_hi + s4_ext[pl.ds(POOL_HALO - 4, TOK), :]
    s8_ext[POOL_HALO:POOL_HALO + TOK, :] = s8
    s16 = s8 + s8_ext[pl.ds(POOL_HALO - 8, TOK), :]
    lane = lax.broadcasted_iota(jnp.int32, (1, LANES), 1)
    wsum = jnp.concatenate([jnp.where(lane < B_GROUP_DIM, s2[:, :LANES], s4[:, :LANES]),
                            jnp.where(lane < B_GROUP_DIM, s8, s16)], axis=1)
    lane_b = lax.broadcasted_iota(jnp.int32, (1, B_WIDTH), 1) // B_GROUP_DIM
    win = jnp.where(lane_b == 0, POOL_WINDOWS[0],
                    jnp.where(lane_b == 1, POOL_WINDOWS[1],
                              jnp.where(lane_b == 2, POOL_WINDOWS[2], POOL_WINDOWS[3])))
    inv_win = 1.0 / win.astype(_F32)
    t_head = cm * TOK + lax.broadcasted_iota(jnp.int32, (POOL_HALO, B_WIDTH), 0)
    cnt_head = jnp.minimum(t_head + 1, win).astype(_F32)
    mean = jnp.concatenate([wsum[0:POOL_HALO] / cnt_head, wsum[POOL_HALO:] * inv_win], axis=0)
    pooled = (mean - bx).astype(_BF16)
    for ref in (xb_ext, s2_ext, s4_ext, s8_ext):
        ref[0:POOL_HALO, :] = ref[TOK:TOK + POOL_HALO, :]
    yield
    yb = jnp.dot(pooled, p["pool_w"][...], preferred_element_type=_F32) + p["pool_b"][...]
    yb = yb * p["pool_scale"][...]
    yb = _rmsnorm(yb, bng[:, A_WIDTH:A_WIDTH + B_WIDTH])
    yb = yb * _silu(proj_ref[:, OFF_B + B_WIDTH:OFF_B + 2 * B_WIDTH])
    y_ref[:, A_WIDTH:A_WIDTH + B_WIDTH] = yb.astype(_BF16)
    yield

    cw = p["conv_w"][...]
    neg_c_softplus = -LRU_C * jax.nn.softplus(-p["lam"][...])
    seg_in, seg_out = s["seg_in"], s["seg_out"]
    all_slabs = tuple(range(C_WIDTH // LANES))
    _seg_store(seg_in, proj_ref, OFF_C, C_WIDTH, cx_halo)
    s["cx_halo"][...] = proj_ref[TOK - CONV_HALO:TOK, OFF_C:OFF_C + C_WIDTH]
    taps = [jnp.broadcast_to(cw[k:k + 1, :], (SEGS, C_WIDTH)) for k in range(CONV_WIDTH)]
    cbias = jnp.broadcast_to(p["conv_b"][...], (SEGS, C_WIDTH))
    xrow = {r: _seg_row(seg_in, r, all_slabs) for r in range(CONV_HALO - (CONV_WIDTH - 1), SEG_PITCH)}
    conv_rows = []
    for j in range(SEG_ROWS):
        acc = cbias + xrow[j + CONV_HALO] * taps[CONV_WIDTH - 1]
        for k in range(CONV_WIDTH - 1):
            acc = acc + xrow[j + CONV_HALO - (CONV_WIDTH - 1 - k)] * taps[k]
        conv_rows.append(acc)
    conv_all = jnp.concatenate(conv_rows, axis=0)
    yield
    for d in range(2):
        lo = d * C_HALF
        conv = conv_all[:, lo:lo + C_HALF]
        gates = jnp.dot(conv.astype(_BF16), p["gate_w"][d], preferred_element_type=_F32)
        r = _sigmoid(gates[:, 0:C_HALF] + p["b_a"][:, lo:lo + C_HALF])
        i = _sigmoid(gates[:, C_HALF:] + p["b_x"][:, lo:lo + C_HALF])
        log_a = r * neg_c_softplus[:, lo:lo + C_HALF]
        a = jnp.exp(log_a)
        msq = -jnp.tanh(log_a) * (a * a + 1.0)
        mult = jnp.where(msq > 0.0, msq * lax.rsqrt(msq), 0.0)
        bterm = mult * (i * conv)
        yield
        slabs = all_slabs[d * (C_HALF // LANES):(d + 1) * (C_HALF // LANES)]
        s["h_carry"][:, lo:lo + C_HALF] = _segment_scan(a, bterm, hc[:, lo:lo + C_HALF], seg_out, slabs)
        yield
    yc = _rmsnorm(_seg_load_time_order(seg_out, all_slabs), bng[:, A_WIDTH + B_WIDTH:])
    yc = yc * _silu(proj_ref[:, OFF_C + C_WIDTH:OFF_C + 2 * C_WIDTH])
    y_ref[:, A_WIDTH + B_WIDTH:] = yc.astype(_BF16)
    yield


def _matmul_tiles(lhs_ref, w_ref, n_cols, emit):
    def tile(c0):
        cols = slice(c0, c0 + MXU_COLS)
        emit(cols, jnp.dot(lhs_ref[...], w_ref[:, cols], preferred_element_type=_F32))
    return [functools.partial(tile, c0) for c0 in range(0, n_cols, MXU_COLS)]


_PARAM_NAMES = ("norm_g", "w_in", "sgu_g", "sgu_w", "sgu_bias", "pool_w", "pool_b", "pool_scale",
                "conv_w", "conv_b", "gate_w", "b_a", "b_x", "lam", "bng", "w_out", "final_g")
_STATE_NAMES = ("xb_ext", "s2_ext", "s4_ext", "s8_ext", "cx_halo", "h_carry", "seg_in", "seg_out")


def _layer_kernel(*refs, is_last, chunks_per_seq):
    x_ref, xlag_ref = refs[0], refs[1]
    p = dict(zip(_PARAM_NAMES, refs[2:2 + len(_PARAM_NAMES)]))
    o_ref = refs[2 + len(_PARAM_NAMES)]
    scratch = refs[3 + len(_PARAM_NAMES):]
    proj_bufs = scratch[0:PAIR]
    y_bufs = scratch[PAIR:2 * PAIR]
    hn_buf = scratch[2 * PAIR]
    s = dict(zip(_STATE_NAMES, scratch[2 * PAIR + 1:]))
    k = pl.program_id(0)

    @pl.when(k == 0)
    def _init():
        for ref in scratch:
            ref[...] = jnp.zeros(ref.shape, ref.dtype)

    for j in range(PAIR):
        rows = slice(j * TOK, (j + 1) * TOK)
        proj_ref = proj_bufs[j]

        def emit_out(cols, t, rows=rows):
            o_ref[rows, cols] = xlag_ref[rows, cols] + t

        mxu_work = _matmul_tiles(y_bufs[j], p["w_out"], D_MODEL, emit_out)

        def emit_proj(cols, t, proj_ref=proj_ref):
            proj_ref[:, cols] = t

        mxu_work += _matmul_tiles(hn_buf, p["w_in"], D_IN, emit_proj)

        cm = lax.rem(PAIR * k + j + chunks_per_seq - 1, chunks_per_seq)
        mix = _mix_pieces(proj_bufs[1 - j], y_bufs[1 - j], cm, p, s)

        n_out_tiles = D_MODEL // MXU_COLS
        for t, thunk in enumerate(mxu_work):
            if t == 0:
                hn_buf[...] = _rmsnorm(x_ref[rows, :], p["norm_g"][...]).astype(_BF16)
            else:
                next(mix, None)
            thunk()
            if is_last and t == n_out_tiles - 1:
                o_ref[rows, :] = _rmsnorm(o_ref[rows, :], p["final_g"][...])
        for _ in mix:
            pass


def _block_diag(w):
    n, d, _ = w.shape
    eye = jnp.eye(n, dtype=w.dtype)
    return (eye[:, None, :, None] * w[:, :, None, :]).reshape(n * d, n * d)


def _layer(x, norm_g, w_in, sgu_g, sgu_w, sgu_b, pool_w, pool_b, pool_scale, conv_w, conv_b,
           lru_wa, lru_ba, lru_wx, lru_bx, lam, bng, w_out, final_g, *, is_last, name):
    bsz, seq, _ = x.shape
    step_rows = PAIR * TOK
    assert seq % step_rows == 0 and TOK % SGU_BLOCK == 0 and PAIR == 2
    steps_per_seq = seq // step_rows
    n_real = bsz * steps_per_seq
    row = lambda a: a.reshape(1, -1).astype(_F32)
    heads_per_half = C_HEADS // 2
    gate_w = jnp.stack([
        jnp.concatenate([_block_diag(lru_wa[d * heads_per_half:(d + 1) * heads_per_half]),
                         _block_diag(lru_wx[d * heads_per_half:(d + 1) * heads_per_half])], axis=1)
        for d in range(2)]).astype(_BF16)
    sgu_bias = jnp.repeat(sgu_b.T, A_HEAD_DIM, axis=1).astype(_F32)
    params = dict(
        norm_g=row(norm_g), w_in=w_in.astype(_BF16), sgu_g=row(sgu_g), sgu_w=sgu_w.astype(_F32),
        sgu_bias=sgu_bias, pool_w=_block_diag(pool_w).astype(_BF16), pool_b=row(pool_b),
        pool_scale=row(pool_scale), conv_w=conv_w.astype(_F32), conv_b=row(conv_b), gate_w=gate_w,
        b_a=row(lru_ba), b_x=row(lru_bx), lam=row(lam), bng=row(bng), w_out=w_out.astype(_BF16),
        final_g=row(final_g))
    operands = (x, x) + tuple(params[n] for n in _PARAM_NAMES)

    def x_map(k):
        kk = jnp.minimum(k, n_real - 1)
        return (kk // steps_per_seq, kk % steps_per_seq, 0)

    def lag_map(k):
        kl = jnp.maximum(k - LAG_STEPS, 0)
        return (kl // steps_per_seq, kl % steps_per_seq, 0)

    def full(shape):
        return pl.BlockSpec(shape, lambda k: (0,) * len(shape))

    in_specs = [pl.BlockSpec((None, step_rows, D_MODEL), x_map),
                pl.BlockSpec((None, step_rows, D_MODEL), lag_map)]
    in_specs += [full(op.shape) for op in operands[2:]]
    scratch_shapes = (
        [pltpu.VMEM((TOK, D_IN), _F32) for _ in range(PAIR)]
        + [pltpu.VMEM((TOK, D_MODEL), _BF16) for _ in range(PAIR)]
        + [pltpu.VMEM((TOK, D_MODEL), _BF16),
           pltpu.VMEM((TOK + POOL_HALO, B_WIDTH), _F32),
           pltpu.VMEM((TOK + POOL_HALO, B_WIDTH), _F32),
           pltpu.VMEM((TOK + POOL_HALO, LANES), _F32),
           pltpu.VMEM((TOK + POOL_HALO, LANES), _F32),
           pltpu.VMEM((CONV_HALO, C_WIDTH), _F32),
           pltpu.VMEM((SUBLANES, C_WIDTH), _F32),
           pltpu.VMEM((C_WIDTH // LANES, SEGS * SEG_PITCH, LANES), _F32),
           pltpu.VMEM((C_WIDTH // LANES, SEGS * SEG_PITCH, LANES), _F32)])
    return pl.pallas_call(
        functools.partial(_layer_kernel, is_last=is_last, chunks_per_seq=seq // TOK),
        out_shape=jax.ShapeDtypeStruct(x.shape, x.dtype),
        grid=(n_real + LAG_STEPS,),
        in_specs=in_specs,
        out_specs=pl.BlockSpec((None, step_rows, D_MODEL), lag_map),
        scratch_shapes=scratch_shapes,
        compiler_params=pltpu.CompilerParams(
            dimension_semantics=("arbitrary",),
            vmem_limit_bytes=VMEM_LIMIT_BYTES,
        ),
        name=name,
    )(*operands)


def kernel(x, norm_g, w_in, sgu_norm_g, sgu_w, sgu_b, pool_w, pool_b, pool_scale, conv_w, conv_b,
           lru_wa, lru_ba, lru_wx, lru_bx, lru_lambda, branch_norm_g, w_out, final_g):
    for l in range(DEPTH):
        x = _layer(x, norm_g[l], w_in[l], sgu_norm_g[l], sgu_w[l], sgu_b[l], pool_w[l], pool_b[l],
                   pool_scale[l], conv_w[l], conv_b[l], lru_wa[l], lru_ba[l], lru_wx[l], lru_bx[l],
                   lru_lambda[l], branch_norm_g[l], w_out[l], final_g,
                   is_last=(l == DEPTH - 1), name=f"hybrid_layer{l}")
    return x
```

```python
import functools

import jax
import jax.numpy as jnp
from jax import lax
from jax.experimental import pallas as pl
from jax.experimental.pallas import tpu as pltpu

D_MODEL = 1024
DEPTH = 2
CHUNK = 64
EPS = 1e-6
A_WIDTH = 256
A_HEADS = 4
A_HEAD_DIM = 64
SGU_BLOCK = 128
B_WIDTH = 256
POOL_WINDOWS = (2, 4, 8, 16)
B_GROUP_DIM = 64
C_WIDTH = 512
C_HEADS = 8
C_HEAD_DIM = 64
CONV_WIDTH = 4
LRU_C = 8.0
D_IN = 3 * A_WIDTH + 2 * B_WIDTH + 2 * C_WIDTH
OFF_B = 3 * A_WIDTH
OFF_C = OFF_B + 2 * B_WIDTH

SUBLANES = 8
LANES = 128
MXU_COLS = 256
TOK = 256
STEP_CHUNKS = 4
PROJ_SLOTS = 2
LAG_STEPS = 1
POOL_HALO = 16
CONV_HALO = 8
C_HALF = C_WIDTH // 2
SEGS = SUBLANES
SEG_ROWS = TOK // SEGS
SEG_PITCH = SEG_ROWS + CONV_HALO
VMEM_LIMIT_BYTES = 56 * 1024 * 1024

_BF16 = jnp.bfloat16
_F32 = jnp.float32


def _rmsnorm(x, g):
    ms = jnp.mean(x * x, axis=-1, keepdims=True)
    return x * lax.rsqrt(ms + EPS) * g


def _silu_of_half(h):
    return h * jnp.tanh(h) + h


def _seg_store(seg_ref, x_ref, col0, width, halo):
    for slab in range(width // LANES):
        lanes = slice(col0 + slab * LANES, col0 + (slab + 1) * LANES)
        for g in range(SEGS):
            base = g * SEG_PITCH
            if g == 0:
                before = halo[:, slab * LANES:(slab + 1) * LANES]
            else:
                before = x_ref[g * SEG_ROWS - CONV_HALO:g * SEG_ROWS, lanes]
            seg_ref[slab, base:base + CONV_HALO, :] = before
            seg_ref[slab, base + CONV_HALO:base + SEG_PITCH, :] = x_ref[g * SEG_ROWS:(g + 1) * SEG_ROWS, lanes]


def _seg_row(seg_ref, r, slabs):
    return jnp.concatenate([seg_ref[slab, pl.ds(r, SEGS, stride=SEG_PITCH), :] for slab in slabs], axis=1)


def _segment_scan(a, b, h0, seg_ref, slabs):
    width = a.shape[-1]
    a3 = a.reshape(SEG_ROWS, SEGS, width)
    b3 = b.reshape(SEG_ROWS, SEGS, width)
    acum, hloc = [a3[0]], [b3[0]]
    for j in range(1, SEG_ROWS):
        acum.append(acum[-1] * a3[j])
        hloc.append(a3[j] * hloc[-1] + b3[j])
    end_a, end_h = acum[-1], hloc[-1]
    sub = lax.broadcasted_iota(jnp.int32, (SEGS, width), 0)
    carry = h0
    for _ in range(SEGS - 1):
        carry = jnp.where(sub == 0, h0, pltpu.roll(end_h + end_a * carry, 1, axis=0))
    for j in range(SEG_ROWS):
        hj = hloc[j] + acum[j] * carry
        for n, slab in enumerate(slabs):
            seg_ref[slab, pl.ds(j, SEGS, stride=SEG_PITCH), :] = hj[:, n * LANES:(n + 1) * LANES]
    last = end_h + end_a * carry
    return jnp.broadcast_to(last[SEGS - 1:SEGS, :], (SEGS, width))


def _seg_load_time_order(seg_ref, slabs):
    return jnp.concatenate(
        [jnp.concatenate([seg_ref[slab, g * SEG_PITCH:g * SEG_PITCH + SEG_ROWS, :] for g in range(SEGS)], axis=0)
         for slab in slabs], axis=1)


def _mix_pieces(proj_ref, y_ref, cm, p, s):
    first = cm == 0
    bng = p["bng"][...]

    for name in ("xb_ext", "s2_ext", "s4_ext", "s8_ext"):
        ref = s[name]
        ref[0:POOL_HALO, :] = jnp.where(first, 0.0, ref[0:POOL_HALO, :])
    cx_halo = jnp.where(first, 0.0, s["cx_halo"][...])
    hc = jnp.where(first, 0.0, s["h_carry"][...])

    v = _rmsnorm(proj_ref[:, A_WIDTH:2 * A_WIDTH], p["sgu_g"][...]).astype(_BF16)
    ci = lax.broadcasted_iota(jnp.int32, (SGU_BLOCK, SGU_BLOCK), 0) // CHUNK
    cj = lax.broadcasted_iota(jnp.int32, (SGU_BLOCK, SGU_BLOCK), 1) // CHUNK
    mask = (cj <= ci).astype(_F32)
    wm = jnp.concatenate([(p["sgu_w"][h] * mask).astype(_BF16) for h in range(A_HEADS)], axis=1)
    head_of_lane = lax.broadcasted_iota(jnp.int32, (1, A_WIDTH), 1) // A_HEAD_DIM
    ys = []
    for n in range(TOK // SGU_BLOCK):
        vn = v[n * SGU_BLOCK:(n + 1) * SGU_BLOCK, :]
        rhs = jnp.concatenate([jnp.where(head_of_lane == h, vn, jnp.zeros_like(vn)) for h in range(A_HEADS)], axis=0)
        ys.append(jnp.dot(wm, rhs, preferred_element_type=_F32) + p["sgu_bias"][...])
    ya = proj_ref[:, 0:A_WIDTH] * jnp.concatenate(ys, axis=0)
    yield
    ya = _rmsnorm(ya, bng[:, 0:A_WIDTH]) * _silu_of_half(proj_ref[:, 2 * A_WIDTH:3 * A_WIDTH])
    y_ref[:, 0:A_WIDTH] = ya.astype(_BF16)
    yield

    xb_ext, s2_ext, s4_ext, s8_ext = s["xb_ext"], s["s2_ext"], s["s4_ext"], s["s8_ext"]
    bx = proj_ref[:, OFF_B:OFF_B + B_WIDTH]
    xb_ext[POOL_HALO:POOL_HALO + TOK, :] = bx
    s2 = bx + xb_ext[pl.ds(POOL_HALO - 1, TOK), :]
    s2_ext[POOL_HALO:POOL_HALO + TOK, :] = s2
    s4 = s2 + s2_ext[pl.ds(POOL_HALO - 2, TOK), :]
    s4_hi = s4[:, LANES:]
    s4_ext[POOL_HALO:POOL_HALO + TOK, :] = s4_hi
    s8 = s4_hi + s4_ext[pl.ds(POOL_HALO - 4, TOK), :]
    s8_ext[POOL_HALO:POOL_HALO + TOK, :] = s8
    s16 = s8 + s8_ext[pl.ds(POOL_HALO - 8, TOK), :]
    lane = lax.broadcasted_iota(jnp.int32, (1, LANES), 1)
    wsum = jnp.concatenate([jnp.where(lane < B_GROUP_DIM, s2[:, :LANES], s4[:, :LANES]),
                            jnp.where(lane < B_GROUP_DIM, s8, s16)], axis=1)
    lane_b = lax.broadcasted_iota(jnp.int32, (1, B_WIDTH), 1) // B_GROUP_DIM
    win = jnp.where(lane_b == 0, POOL_WINDOWS[0],
                    jnp.where(lane_b == 1, POOL_WINDOWS[1],
                              jnp.where(lane_b == 2, POOL_WINDOWS[2], POOL_WINDOWS[3])))
    inv_win = 1.0 / win.astype(_F32)
    t_head = cm * TOK + lax.broadcasted_iota(jnp.int32, (POOL_HALO, B_WIDTH), 0)
    cnt_head = jnp.minimum(t_head + 1, win).astype(_F32)
    mean = jnp.concatenate([wsum[0:POOL_HALO] / cnt_head, wsum[POOL_HALO:] * inv_win], axis=0)
    pooled = (mean - bx).astype(_BF16)
    for ref in (xb_ext, s2_ext, s4_ext, s8_ext):
        ref[0:POOL_HALO, :] = ref[TOK:TOK + POOL_HALO, :]
    yield
    yb = jnp.dot(pooled, p["pool_w"][...], preferred_element_type=_F32) + p["pool_b"][...]
    yb = yb * p["pool_scale"][...]
    yb = _rmsnorm(yb, bng[:, A_WIDTH:A_WIDTH + B_WIDTH])
    yb = yb * _silu_of_half(proj_ref[:, OFF_B + B_WIDTH:OFF_B + 2 * B_WIDTH])
    y_ref[:, A_WIDTH:A_WIDTH + B_WIDTH] = yb.astype(_BF16)
    yield

    cw = p["conv_w"][...]
    half_neg_c_softplus = (-0.5 * LRU_C) * jax.nn.softplus(-p["lam"][...])
    seg_in, seg_out = s["seg_in"], s["seg_out"]
    all_slabs = tuple(range(C_WIDTH // LANES))
    _seg_store(seg_in, proj_ref, OFF_C, C_WIDTH, cx_halo)
    s["cx_halo"][...] = proj_ref[TOK - CONV_HALO:TOK, OFF_C:OFF_C + C_WIDTH]
    taps = [jnp.broadcast_to(cw[k:k + 1, :], (SEGS, C_WIDTH)) for k in range(CONV_WIDTH)]
    cbias = jnp.broadcast_to(p["conv_b"][...], (SEGS, C_WIDTH))
    xrow = {r: _seg_row(seg_in, r, all_slabs) for r in range(CONV_HALO - (CONV_WIDTH - 1), SEG_PITCH)}
    conv_rows = []
    for j in range(SEG_ROWS):
        acc = cbias + xrow[j + CONV_HALO] * taps[CONV_WIDTH - 1]
        for k in range(CONV_WIDTH - 1):
            acc = acc + xrow[j + CONV_HALO - (CONV_WIDTH - 1 - k)] * taps[k]
        conv_rows.append(acc)
    conv_all = jnp.concatenate(conv_rows, axis=0)
    yield
    for d in range(2):
        lo = d * C_HALF
        conv = conv_all[:, lo:lo + C_HALF]
        half_pre = jnp.dot(conv.astype(_BF16), p["gate_w"][d], preferred_element_type=_F32)
        t_r = jnp.tanh(half_pre[:, 0:C_HALF] + p["b_a"][:, lo:lo + C_HALF])
        t_i = jnp.tanh(half_pre[:, C_HALF:] + p["b_x"][:, lo:lo + C_HALF])
        half_nsp = half_neg_c_softplus[:, lo:lo + C_HALF]
        log_a = half_nsp * t_r + half_nsp
        a = jnp.exp(log_a)
        msq = -jnp.tanh(log_a) * (a * a + 1.0)
        mult = jnp.where(msq > 0.0, msq * lax.rsqrt(msq), 0.0)
        half_conv = 0.5 * conv
        bterm = mult * (half_conv * t_i + half_conv)
        yield
        slabs = all_slabs[d * (C_HALF // LANES):(d + 1) * (C_HALF // LANES)]
        s["h_carry"][:, lo:lo + C_HALF] = _segment_scan(a, bterm, hc[:, lo:lo + C_HALF], seg_out, slabs)
        yield
    yc = _rmsnorm(_seg_load_time_order(seg_out, all_slabs), bng[:, A_WIDTH + B_WIDTH:])
    yc = yc * _silu_of_half(proj_ref[:, OFF_C + C_WIDTH:OFF_C + 2 * C_WIDTH])
    y_ref[:, A_WIDTH + B_WIDTH:] = yc.astype(_BF16)
    yield


def _matmul_tiles(lhs_ref, w_ref, emit):
    def tile(ti):
        cols = slice(ti * MXU_COLS, (ti + 1) * MXU_COLS)
        emit(cols, jnp.dot(lhs_ref[...], w_ref[ti], preferred_element_type=_F32))
    return [functools.partial(tile, ti) for ti in range(w_ref.shape[0])]


_PARAM_NAMES = ("norm_g", "w_in", "sgu_g", "sgu_w", "sgu_bias", "pool_w", "pool_b", "pool_scale",
                "conv_w", "conv_b", "gate_w", "b_a", "b_x", "lam", "bng", "w_out", "final_g")
_STATE_NAMES = ("xb_ext", "s2_ext", "s4_ext", "s8_ext", "cx_halo", "h_carry", "seg_in", "seg_out")


def _layer_kernel(*refs, is_last, chunks_per_seq):
    x_ref, xlag_ref = refs[0], refs[1]
    p = dict(zip(_PARAM_NAMES, refs[2:2 + len(_PARAM_NAMES)]))
    o_ref = refs[2 + len(_PARAM_NAMES)]
    scratch = refs[3 + len(_PARAM_NAMES):]
    proj_bufs = scratch[0:PROJ_SLOTS]
    y_bufs = scratch[PROJ_SLOTS:PROJ_SLOTS + STEP_CHUNKS]
    hn_buf = scratch[PROJ_SLOTS + STEP_CHUNKS]
    s = dict(zip(_STATE_NAMES, scratch[PROJ_SLOTS + STEP_CHUNKS + 1:]))
    k = pl.program_id(0)

    @pl.when(k == 0)
    def _init():
        for ref in scratch:
            ref[...] = jnp.zeros(ref.shape, ref.dtype)

    for j in range(STEP_CHUNKS):
        rows = slice(j * TOK, (j + 1) * TOK)
        proj_ref = proj_bufs[j % PROJ_SLOTS]

        def emit_out(cols, t, rows=rows):
            o_ref[rows, cols] = xlag_ref[rows, cols] + t

        mxu_work = _matmul_tiles(y_bufs[j], p["w_out"], emit_out)

        def emit_proj(cols, t, proj_ref=proj_ref):
            proj_ref[:, cols] = t

        mxu_work += _matmul_tiles(hn_buf, p["w_in"], emit_proj)

        cm = lax.rem(STEP_CHUNKS * k + j + chunks_per_seq - 1, chunks_per_seq)
        mix = _mix_pieces(proj_bufs[(j - 1) % PROJ_SLOTS], y_bufs[(j - 1) % STEP_CHUNKS], cm, p, s)

        n_out_tiles = D_MODEL // MXU_COLS
        for t, thunk in enumerate(mxu_work):
            if t == 0:
                hn_buf[...] = _rmsnorm(x_ref[rows, :], p["norm_g"][...]).astype(_BF16)
            else:
                next(mix, None)
            thunk()
            if is_last and t == n_out_tiles - 1:
                o_ref[rows, :] = _rmsnorm(o_ref[rows, :], p["final_g"][...])
        for _ in mix:
            pass


def _block_diag(w):
    n, d, _ = w.shape
    eye = jnp.eye(n, dtype=w.dtype)
    return (eye[:, None, :, None] * w[:, :, None, :]).reshape(n * d, n * d)


def _layer(x, norm_g, w_in, sgu_g, sgu_w, sgu_b, pool_w, pool_b, pool_scale, conv_w, conv_b,
           lru_wa, lru_ba, lru_wx, lru_bx, lam, bng, w_out, final_g, *, is_last, name):
    bsz, seq, _ = x.shape
    step_rows = STEP_CHUNKS * TOK
    assert seq % step_rows == 0 and TOK % SGU_BLOCK == 0 and STEP_CHUNKS % PROJ_SLOTS == 0
    steps_per_seq = seq // step_rows
    n_real = bsz * steps_per_seq
    row = lambda a: a.reshape(1, -1).astype(_F32)
    heads_per_half = C_HEADS // 2
    gate_w = jnp.stack([
        jnp.concatenate([_block_diag(lru_wa[d * heads_per_half:(d + 1) * heads_per_half]),
                         _block_diag(lru_wx[d * heads_per_half:(d + 1) * heads_per_half])], axis=1)
        for d in range(2)])
    half_gate_cols = jnp.concatenate([
        jnp.ones((2 * A_WIDTH,)), jnp.full((A_WIDTH,), 0.5), jnp.ones((B_WIDTH,)), jnp.full((B_WIDTH,), 0.5),
        jnp.ones((C_WIDTH,)), jnp.full((C_WIDTH,), 0.5)]).astype(_F32)

    def col_tiles(w):
        return w.reshape(w.shape[0], -1, MXU_COLS).transpose(1, 0, 2).astype(_BF16)
    sgu_bias = jnp.repeat(sgu_b.T, A_HEAD_DIM, axis=1).astype(_F32)
    params = dict(
        norm_g=row(norm_g), w_in=col_tiles(w_in * half_gate_cols), sgu_g=row(sgu_g), sgu_w=sgu_w.astype(_F32),
        sgu_bias=sgu_bias, pool_w=_block_diag(pool_w).astype(_BF16), pool_b=row(pool_b),
        pool_scale=row(pool_scale), conv_w=conv_w.astype(_F32), conv_b=row(conv_b),
        gate_w=(0.5 * gate_w).astype(_BF16), b_a=row(0.5 * lru_ba), b_x=row(0.5 * lru_bx), lam=row(lam),
        bng=row(bng), w_out=col_tiles(w_out),
        final_g=row(final_g))
    operands = (x, x) + tuple(params[n] for n in _PARAM_NAMES)

    def x_map(k):
        kk = jnp.minimum(k, n_real - 1)
        return (kk // steps_per_seq, kk % steps_per_seq, 0)

    def lag_map(k):
        kl = jnp.maximum(k - LAG_STEPS, 0)
        return (kl // steps_per_seq, kl % steps_per_seq, 0)

    def full(shape):
        return pl.BlockSpec(shape, lambda k: (0,) * len(shape), pipeline_mode=pl.Buffered(1))

    in_specs = [pl.BlockSpec((None, step_rows, D_MODEL), x_map),
                pl.BlockSpec((None, step_rows, D_MODEL), lag_map)]
    in_specs += [full(op.shape) for op in operands[2:]]
    scratch_shapes = (
        [pltpu.VMEM((TOK, D_IN), _F32) for _ in range(PROJ_SLOTS)]
        + [pltpu.VMEM((TOK, D_MODEL), _BF16) for _ in range(STEP_CHUNKS)]
        + [pltpu.VMEM((TOK, D_MODEL), _BF16),
           pltpu.VMEM((TOK + POOL_HALO, B_WIDTH), _F32),
           pltpu.VMEM((TOK + POOL_HALO, B_WIDTH), _F32),
           pltpu.VMEM((TOK + POOL_HALO, LANES), _F32),
           pltpu.VMEM((TOK + POOL_HALO, LANES), _F32),
           pltpu.VMEM((CONV_HALO, C_WIDTH), _F32),
           pltpu.VMEM((SUBLANES, C_WIDTH), _F32),
           pltpu.VMEM((C_WIDTH // LANES, SEGS * SEG_PITCH, LANES), _F32),
           pltpu.VMEM((C_WIDTH // LANES, SEGS * SEG_PITCH, LANES), _F32)])
    return pl.pallas_call(
        functools.partial(_layer_kernel, is_last=is_last, chunks_per_seq=seq // TOK),
        out_shape=jax.ShapeDtypeStruct(x.shape, x.dtype),
        grid=(n_real + LAG_STEPS,),
        in_specs=in_specs,
        out_specs=pl.BlockSpec((None, step_rows, D_MODEL), lag_map),
        scratch_shapes=scratch_shapes,
        compiler_params=pltpu.CompilerParams(
            dimension_semantics=("arbitrary",),
            vmem_limit_bytes=VMEM_LIMIT_BYTES,
        ),
        name=name,
    )(*operands)


def kernel(x, norm_g, w_in, sgu_norm_g, sgu_w, sgu_b, pool_w, pool_b, pool_scale, conv_w, conv_b,
           lru_wa, lru_ba, lru_wx, lru_bx, lru_lambda, branch_norm_g, w_out, final_g):
    for l in range(DEPTH):
        x = _layer(x, norm_g[l], w_in[l], sgu_norm_g[l], sgu_w[l], sgu_b[l], pool_w[l], pool_b[l],
                   pool_scale[l], conv_w[l], conv_b[l], lru_wa[l], lru_ba[l], lru_wx[l], lru_bx[l],
                   lru_lambda[l], branch_norm_g[l], w_out[l], final_g,
                   is_last=(l == DEPTH - 1), name=f"hybrid_layer{l}")
    return x
```

```python
import functools

import jax
import jax.numpy as jnp
from jax import lax
from jax.experimental import pallas as pl
from jax.experimental.pallas import tpu as pltpu

D_MODEL = 1024
DEPTH = 2
CHUNK = 64
EPS = 1e-6
A_WIDTH = 256
A_HEADS = 4
A_HEAD_DIM = 64
SGU_BLOCK = 128
B_WIDTH = 256
POOL_WINDOWS = (2, 4, 8, 16)
B_GROUP_DIM = 64
C_WIDTH = 512
C_HEADS = 8
C_HEAD_DIM = 64
CONV_WIDTH = 4
LRU_C = 8.0
D_IN = 3 * A_WIDTH + 2 * B_WIDTH + 2 * C_WIDTH
OFF_B = 3 * A_WIDTH
OFF_C = OFF_B + 2 * B_WIDTH

SUBLANES = 8
LANES = 128
MXU_COLS = 256
TOK = 256
STEP_CHUNKS = 4
PROJ_SLOTS = 2
LAG_STEPS = 1
POOL_HALO = 16
CONV_HALO = 8
C_HALF = C_WIDTH // 2
SEGS = SUBLANES
SEG_ROWS = TOK // SEGS
SEG_PITCH = SEG_ROWS + CONV_HALO
VMEM_LIMIT_BYTES = 56 * 1024 * 1024

_BF16 = jnp.bfloat16
_F32 = jnp.float32


def _rmsnorm(x, g):
    ms = jnp.mean(x * x, axis=-1, keepdims=True)
    return x * lax.rsqrt(ms + EPS) * g


def _silu_of_half(h):
    return h * jnp.tanh(h) + h


def _seg_store(seg_ref, x_ref, col0, width, halo):
    for slab in range(width // LANES):
        lanes = slice(col0 + slab * LANES, col0 + (slab + 1) * LANES)
        for g in range(SEGS):
            base = g * SEG_PITCH
            if g == 0:
                before = halo[:, slab * LANES:(slab + 1) * LANES]
            else:
                before = x_ref[g * SEG_ROWS - CONV_HALO:g * SEG_ROWS, lanes]
            seg_ref[slab, base:base + CONV_HALO, :] = before
            seg_ref[slab, base + CONV_HALO:base + SEG_PITCH, :] = x_ref[g * SEG_ROWS:(g + 1) * SEG_ROWS, lanes]


def _seg_row(seg_ref, r, slabs):
    return jnp.concatenate([seg_ref[slab, pl.ds(r, SEGS, stride=SEG_PITCH), :] for slab in slabs], axis=1)


def _segment_scan(a, b, h0, seg_ref, slabs):
    width = a.shape[-1]
    a3 = a.reshape(SEG_ROWS, SEGS, width)
    b3 = b.reshape(SEG_ROWS, SEGS, width)
    acum, hloc = [a3[0]], [b3[0]]
    for j in range(1, SEG_ROWS):
        acum.append(acum[-1] * a3[j])
        hloc.append(a3[j] * hloc[-1] + b3[j])
    end_a, end_h = acum[-1], hloc[-1]
    sub = lax.broadcasted_iota(jnp.int32, (SEGS, width), 0)
    carry = h0
    for _ in range(SEGS - 1):
        carry = jnp.where(sub == 0, h0, pltpu.roll(end_h + end_a * carry, 1, axis=0))
    for j in range(SEG_ROWS):
        hj = hloc[j] + acum[j] * carry
        for n, slab in enumerate(slabs):
            seg_ref[slab, pl.ds(j, SEGS, stride=SEG_PITCH), :] = hj[:, n * LANES:(n + 1) * LANES]
    last = end_h + end_a * carry
    return jnp.broadcast_to(last[SEGS - 1:SEGS, :], (SEGS, width))


def _seg_load_time_order(seg_ref, slabs):
    return jnp.concatenate(
        [jnp.concatenate([seg_ref[slab, g * SEG_PITCH:g * SEG_PITCH + SEG_ROWS, :] for g in range(SEGS)], axis=0)
         for slab in slabs], axis=1)


def _mix_pieces(proj_ref, y_ref, cm, p, s):
    first = cm == 0
    bng = p["bng"][...]

    for name in ("xb_ext", "s2_ext", "s4_ext", "s8_ext"):
        ref = s[name]
        ref[0:POOL_HALO, :] = jnp.where(first, 0.0, ref[0:POOL_HALO, :])
    cx_halo = jnp.where(first, 0.0, s["cx_halo"][...])
    hc = jnp.where(first, 0.0, s["h_carry"][...])

    v = _rmsnorm(proj_ref[:, A_WIDTH:2 * A_WIDTH], p["sgu_g"][...]).astype(_BF16)
    ci = lax.broadcasted_iota(jnp.int32, (SGU_BLOCK, SGU_BLOCK), 0) // CHUNK
    cj = lax.broadcasted_iota(jnp.int32, (SGU_BLOCK, SGU_BLOCK), 1) // CHUNK
    mask = (cj <= ci).astype(_F32)
    wm = jnp.concatenate([(p["sgu_w"][h] * mask).astype(_BF16) for h in range(A_HEADS)], axis=1)
    head_of_lane = lax.broadcasted_iota(jnp.int32, (1, A_WIDTH), 1) // A_HEAD_DIM
    ys = []
    for n in range(TOK // SGU_BLOCK):
        vn = v[n * SGU_BLOCK:(n + 1) * SGU_BLOCK, :]
        rhs = jnp.concatenate([jnp.where(head_of_lane == h, vn, jnp.zeros_like(vn)) for h in range(A_HEADS)], axis=0)
        ys.append(jnp.dot(wm, rhs, preferred_element_type=_F32) + p["sgu_bias"][...])
    ya = proj_ref[:, 0:A_WIDTH] * jnp.concatenate(ys, axis=0)
    yield
    ya = _rmsnorm(ya, bng[:, 0:A_WIDTH]) * _silu_of_half(proj_ref[:, 2 * A_WIDTH:3 * A_WIDTH])
    y_ref[:, 0:A_WIDTH] = ya.astype(_BF16)
    yield

    xb_ext, s2_ext, s4_ext, s8_ext = s["xb_ext"], s["s2_ext"], s["s4_ext"], s["s8_ext"]
    bx = proj_ref[:, OFF_B:OFF_B + B_WIDTH]
    xb_ext[POOL_HALO:POOL_HALO + TOK, :] = bx
    s2 = bx + xb_ext[pl.ds(POOL_HALO - 1, TOK), :]
    s2_ext[POOL_HALO:POOL_HALO + TOK, :] = s2
    s4 = s2 + s2_ext[pl.ds(POOL_HALO - 2, TOK), :]
    s4_hi = s4[:, LANES:]
    s4_ext[POOL_HALO:POOL_HALO + TOK, :] = s4_hi
    s8 = s4_hi + s4_ext[pl.ds(POOL_HALO - 4, TOK), :]
    s8_ext[POOL_HALO:POOL_HALO + TOK, :] = s8
    s16 = s8 + s8_ext[pl.ds(POOL_HALO - 8, TOK), :]
    lane = lax.broadcasted_iota(jnp.int32, (1, LANES), 1)
    wsum = jnp.concatenate([jnp.where(lane < B_GROUP_DIM, s2[:, :LANES], s4[:, :LANES]),
                            jnp.where(lane < B_GROUP_DIM, s8, s16)], axis=1)
    lane_b = lax.broadcasted_iota(jnp.int32, (1, B_WIDTH), 1) // B_GROUP_DIM
    win = jnp.where(lane_b == 0, POOL_WINDOWS[0],
                    jnp.where(lane_b == 1, POOL_WINDOWS[1],
                              jnp.where(lane_b == 2, POOL_WINDOWS[2], POOL_WINDOWS[3])))
    inv_win = 1.0 / win.astype(_F32)
    t_head = cm * TOK + lax.broadcasted_iota(jnp.int32, (POOL_HALO, B_WIDTH), 0)
    cnt_head = jnp.minimum(t_head + 1, win).astype(_F32)
    mean = jnp.concatenate([wsum[0:POOL_HALO] / cnt_head, wsum[POOL_HALO:] * inv_win], axis=0)
    pooled = (mean - bx).astype(_BF16)
    for ref in (xb_ext, s2_ext, s4_ext, s8_ext):
        ref[0:POOL_HALO, :] = ref[TOK:TOK + POOL_HALO, :]
    yield
    yb = jnp.dot(pooled, p["pool_w"][...], preferred_element_type=_F32) + p["pool_b"][...]
    yb = yb * p["pool_scale"][...]
    yb = _rmsnorm(yb, bng[:, A_WIDTH:A_WIDTH + B_WIDTH])
    yb = yb * _silu_of_half(proj_ref[:, OFF_B + B_WIDTH:OFF_B + 2 * B_WIDTH])
    y_ref[:, A_WIDTH:A_WIDTH + B_WIDTH] = yb.astype(_BF16)
    yield

    cw = p["conv_w"][...]
    half_neg_c_softplus = (-0.5 * LRU_C) * jax.nn.softplus(-p["lam"][...])
    seg_in, seg_out = s["seg_in"], s["seg_out"]
    all_slabs = tuple(range(C_WIDTH // LANES))
    _seg_store(seg_in, proj_ref, OFF_C, C_WIDTH, cx_halo)
    s["cx_halo"][...] = proj_ref[TOK - CONV_HALO:TOK, OFF_C:OFF_C + C_WIDTH]
    taps = [jnp.broadcast_to(cw[k:k + 1, :], (SEGS, C_WIDTH)) for k in range(CONV_WIDTH)]
    cbias = jnp.broadcast_to(p["conv_b"][...], (SEGS, C_WIDTH))
    xrow = {r: _seg_row(seg_in, r, all_slabs) for r in range(CONV_HALO - (CONV_WIDTH - 1), SEG_PITCH)}
    conv_rows = []
    for j in range(SEG_ROWS):
        acc = cbias + xrow[j + CONV_HALO] * taps[CONV_WIDTH - 1]
        for k in range(CONV_WIDTH - 1):
            acc = acc + xrow[j + CONV_HALO - (CONV_WIDTH - 1 - k)] * taps[k]
        conv_rows.append(acc)
    conv_all = jnp.concatenate(conv_rows, axis=0)
    yield
    for d in range(2):
        lo = d * C_HALF
        conv = conv_all[:, lo:lo + C_HALF]
        half_pre = jnp.dot(conv.astype(_BF16), p["gate_w"][d], preferred_element_type=_F32)
        t_r = jnp.tanh(half_pre[:, 0:C_HALF] + p["b_a"][:, lo:lo + C_HALF])
        t_i = jnp.tanh(half_pre[:, C_HALF:] + p["b_x"][:, lo:lo + C_HALF])
        half_nsp = half_neg_c_softplus[:, lo:lo + C_HALF]
        log_a = half_nsp * t_r + half_nsp
        a = jnp.exp(log_a)
        msq = -jnp.tanh(log_a) * (a * a + 1.0)
        mult = jnp.where(msq > 0.0, msq * lax.rsqrt(msq), 0.0)
        half_conv = 0.5 * conv
        bterm = mult * (half_conv * t_i + half_conv)
        yield
        slabs = all_slabs[d * (C_HALF // LANES):(d + 1) * (C_HALF // LANES)]
        s["h_carry"][:, lo:lo + C_HALF] = _segment_scan(a, bterm, hc[:, lo:lo + C_HALF], seg_out, slabs)
        yield
    yc = _rmsnorm(_seg_load_time_order(seg_out, all_slabs), bng[:, A_WIDTH + B_WIDTH:])
    yc = yc * _silu_of_half(proj_ref[:, OFF_C + C_WIDTH:OFF_C + 2 * C_WIDTH])
    y_ref[:, A_WIDTH + B_WIDTH:] = yc.astype(_BF16)
    yield


def _matmul_tiles(lhs_ref, w_tile_refs, emit):
    def tile(ti):
        cols = slice(ti * MXU_COLS, (ti + 1) * MXU_COLS)
        emit(cols, jnp.dot(lhs_ref[...], w_tile_refs[ti][...], preferred_element_type=_F32))
    return [functools.partial(tile, ti) for ti in range(len(w_tile_refs))]


_PARAM_NAMES = ("norm_g", "sgu_g", "sgu_w", "sgu_bias", "pool_w", "pool_b", "pool_scale",
                "conv_w", "conv_b", "gate_w", "b_a", "b_x", "lam", "bng", "final_g")
N_IN_TILES = D_IN // MXU_COLS
N_OUT_TILES = D_MODEL // MXU_COLS
_STATE_NAMES = ("xb_ext", "s2_ext", "s4_ext", "s8_ext", "cx_halo", "h_carry", "seg_in", "seg_out")


def _layer_kernel(*refs, is_last, chunks_per_seq):
    x_ref, xlag_ref = refs[0], refs[1]
    n = 2 + len(_PARAM_NAMES)
    p = dict(zip(_PARAM_NAMES, refs[2:n]))
    w_in_tiles = refs[n:n + N_IN_TILES]
    w_out_tiles = refs[n + N_IN_TILES:n + N_IN_TILES + N_OUT_TILES]
    n += N_IN_TILES + N_OUT_TILES
    o_ref = refs[n]
    scratch = refs[n + 1:]
    proj_bufs = scratch[0:PROJ_SLOTS]
    y_bufs = scratch[PROJ_SLOTS:PROJ_SLOTS + STEP_CHUNKS]
    hn_buf = scratch[PROJ_SLOTS + STEP_CHUNKS]
    s = dict(zip(_STATE_NAMES, scratch[PROJ_SLOTS + STEP_CHUNKS + 1:]))
    k = pl.program_id(0)

    @pl.when(k == 0)
    def _init():
        for ref in scratch:
            ref[...] = jnp.zeros(ref.shape, ref.dtype)

    for j in range(STEP_CHUNKS):
        rows = slice(j * TOK, (j + 1) * TOK)
        proj_ref = proj_bufs[j % PROJ_SLOTS]

        def emit_out(cols, t, rows=rows):
            o_ref[rows, cols] = xlag_ref[rows, cols] + t

        mxu_work = _matmul_tiles(y_bufs[j], w_out_tiles, emit_out)

        def emit_proj(cols, t, proj_ref=proj_ref):
            proj_ref[:, cols] = t

        mxu_work += _matmul_tiles(hn_buf, w_in_tiles, emit_proj)

        cm = lax.rem(STEP_CHUNKS * k + j + chunks_per_seq - 1, chunks_per_seq)
        mix = _mix_pieces(proj_bufs[(j - 1) % PROJ_SLOTS], y_bufs[(j - 1) % STEP_CHUNKS], cm, p, s)

        for t, thunk in enumerate(mxu_work):
            if t == 0:
                hn_buf[...] = _rmsnorm(x_ref[rows, :], p["norm_g"][...]).astype(_BF16)
            else:
                next(mix, None)
            thunk()
            if is_last and t == N_OUT_TILES - 1:
                o_ref[rows, :] = _rmsnorm(o_ref[rows, :], p["final_g"][...])
        for _ in mix:
            pass


def _block_diag(w):
    n, d = w.shape[-3], w.shape[-2]
    eye = jnp.eye(n, dtype=w.dtype)
    out = eye[:, None, :, None] * w[..., :, :, None, :]
    return out.reshape(w.shape[:-3] + (n * d, n * d))


def _prepare_params(norm_g, w_in, sgu_norm_g, sgu_w, sgu_b, pool_w, pool_b, pool_scale, conv_w, conv_b,
                    lru_wa, lru_ba, lru_wx, lru_bx, lru_lambda, branch_norm_g, w_out, final_g):
    depth = norm_g.shape[0]
    row = lambda a: a.reshape(depth, 1, -1).astype(_F32)
    heads_per_half = C_HEADS // 2
    gate_w = jnp.stack([
        jnp.concatenate([_block_diag(lru_wa[:, d * heads_per_half:(d + 1) * heads_per_half]),
                         _block_diag(lru_wx[:, d * heads_per_half:(d + 1) * heads_per_half])], axis=-1)
        for d in range(2)], axis=1)
    half_gate_cols = jnp.concatenate([
        jnp.ones((2 * A_WIDTH,)), jnp.full((A_WIDTH,), 0.5), jnp.ones((B_WIDTH,)), jnp.full((B_WIDTH,), 0.5),
        jnp.ones((C_WIDTH,)), jnp.full((C_WIDTH,), 0.5)]).astype(_F32)
    sgu_bias = jnp.repeat(jnp.swapaxes(sgu_b, 1, 2), A_HEAD_DIM, axis=2).astype(_F32)
    small = dict(
        norm_g=row(norm_g), sgu_g=row(sgu_norm_g), sgu_w=sgu_w.astype(_F32), sgu_bias=sgu_bias,
        pool_w=_block_diag(pool_w).astype(_BF16), pool_b=row(pool_b), pool_scale=row(pool_scale),
        conv_w=conv_w.astype(_F32), conv_b=row(conv_b), gate_w=(0.5 * gate_w).astype(_BF16),
        b_a=row(0.5 * lru_ba), b_x=row(0.5 * lru_bx), lam=row(lru_lambda), bng=row(branch_norm_g),
        final_g=jnp.broadcast_to(final_g.reshape(1, 1, -1).astype(_F32), (depth, 1, final_g.shape[-1])))
    return small, (w_in * half_gate_cols).astype(_BF16), w_out.astype(_BF16)


def _layer(x, small, w_in_bf, w_out_bf, layer, *, is_last, name):
    bsz, seq, _ = x.shape
    step_rows = STEP_CHUNKS * TOK
    assert seq % step_rows == 0 and TOK % SGU_BLOCK == 0 and STEP_CHUNKS % PROJ_SLOTS == 0
    steps_per_seq = seq // step_rows
    n_real = bsz * steps_per_seq
    small_ops = tuple(small[n] for n in _PARAM_NAMES)
    operands = (x, x) + small_ops + (w_in_bf,) * N_IN_TILES + (w_out_bf,) * N_OUT_TILES

    def x_map(k):
        kk = jnp.minimum(k, n_real - 1)
        return (kk // steps_per_seq, kk % steps_per_seq, 0)

    def lag_map(k):
        kl = jnp.maximum(k - LAG_STEPS, 0)
        return (kl // steps_per_seq, kl % steps_per_seq, 0)

    def of_layer(shape):
        return pl.BlockSpec((None,) + tuple(shape[1:]), lambda k: (layer,) + (0,) * (len(shape) - 1),
                            pipeline_mode=pl.Buffered(1))

    in_specs = [pl.BlockSpec((None, step_rows, D_MODEL), x_map),
                pl.BlockSpec((None, step_rows, D_MODEL), lag_map)]
    in_specs += [of_layer(op.shape) for op in small_ops]
    for w, tiles in ((w_in_bf, N_IN_TILES), (w_out_bf, N_OUT_TILES)):
        in_specs += [pl.BlockSpec((None, w.shape[1], MXU_COLS), functools.partial(lambda k, ti: (layer, 0, ti), ti=ti),
                                  pipeline_mode=pl.Buffered(1)) for ti in range(tiles)]
    scratch_shapes = (
        [pltpu.VMEM((TOK, D_IN), _F32) for _ in range(PROJ_SLOTS)]
        + [pltpu.VMEM((TOK, D_MODEL), _BF16) for _ in range(STEP_CHUNKS)]
        + [pltpu.VMEM((TOK, D_MODEL), _BF16),
           pltpu.VMEM((TOK + POOL_HALO, B_WIDTH), _F32),
           pltpu.VMEM((TOK + POOL_HALO, B_WIDTH), _F32),
           pltpu.VMEM((TOK + POOL_HALO, LANES), _F32),
           pltpu.VMEM((TOK + POOL_HALO, LANES), _F32),
           pltpu.VMEM((CONV_HALO, C_WIDTH), _F32),
           pltpu.VMEM((SUBLANES, C_WIDTH), _F32),
           pltpu.VMEM((C_WIDTH // LANES, SEGS * SEG_PITCH, LANES), _F32),
           pltpu.VMEM((C_WIDTH // LANES, SEGS * SEG_PITCH, LANES), _F32)])
    return pl.pallas_call(
        functools.partial(_layer_kernel, is_last=is_last, chunks_per_seq=seq // TOK),
        out_shape=jax.ShapeDtypeStruct(x.shape, x.dtype),
        grid=(n_real + LAG_STEPS,),
        in_specs=in_specs,
        out_specs=pl.BlockSpec((None, step_rows, D_MODEL), lag_map),
        scratch_shapes=scratch_shapes,
        compiler_params=pltpu.CompilerParams(
            dimension_semantics=("arbitrary",),
            vmem_limit_bytes=VMEM_LIMIT_BYTES,
        ),
        name=name,
    )(*operands)


def kernel(x, norm_g, w_in, sgu_norm_g, sgu_w, sgu_b, pool_w, pool_b, pool_scale, conv_w, conv_b,
           lru_wa, lru_ba, lru_wx, lru_bx, lru_lambda, branch_norm_g, w_out, final_g):
    small, w_in_bf, w_out_bf = _prepare_params(
        norm_g, w_in, sgu_norm_g, sgu_w, sgu_b, pool_w, pool_b, pool_scale, conv_w, conv_b,
        lru_wa, lru_ba, lru_wx, lru_bx, lru_lambda, branch_norm_g, w_out, final_g)
    for l in range(DEPTH):
        x = _layer(x, small, w_in_bf, w_out_bf, l, is_last=(l == DEPTH - 1), name=f"hybrid_layer{l}")
    return x
```

```python
import functools

import jax
import jax.numpy as jnp
from jax import lax
from jax.experimental import pallas as pl
from jax.experimental.pallas import tpu as pltpu

D_MODEL = 1024
DEPTH = 2
CHUNK = 64
EPS = 1e-6
A_WIDTH = 256
A_HEADS = 4
A_HEAD_DIM = 64
SGU_BLOCK = 128
B_WIDTH = 256
POOL_WINDOWS = (2, 4, 8, 16)
B_GROUP_DIM = 64
C_WIDTH = 512
C_HEADS = 8
C_HEAD_DIM = 64
CONV_WIDTH = 4
LRU_C = 8.0
D_IN = 3 * A_WIDTH + 2 * B_WIDTH + 2 * C_WIDTH
OFF_B = 3 * A_WIDTH
OFF_C = OFF_B + 2 * B_WIDTH

SUBLANES = 8
LANES = 128
MXU_COLS = 256
TOK = 256
STEP_CHUNKS = 4
PROJ_SLOTS = 2
LAG_STEPS = 1
POOL_HALO = 16
CONV_HALO = 8
C_HALF = C_WIDTH // 2
SEGS = SUBLANES
SEG_ROWS = TOK // SEGS
SEG_PITCH = SEG_ROWS + CONV_HALO
VMEM_LIMIT_BYTES = 56 * 1024 * 1024

_BF16 = jnp.bfloat16
_F32 = jnp.float32


def _rms_scale(x):
    ms = jnp.mean(x * x, axis=-1, keepdims=True)
    return x * lax.rsqrt(ms + EPS)


def _rmsnorm(x, g):
    return _rms_scale(x) * g


VEC_ROWS = 8
_VEC_SLOTS = {
    "final_g": (0, 0, D_MODEL),
    "sgu_g": (1, 0, A_WIDTH), "pool_b": (1, A_WIDTH, B_WIDTH), "pool_scale": (1, A_WIDTH + B_WIDTH, B_WIDTH),
    "conv_b": (2, 0, C_WIDTH), "lam": (2, C_WIDTH, C_WIDTH),
    "half_b_a": (3, 0, C_WIDTH), "half_b_x": (3, C_WIDTH, C_WIDTH),
}
VEC_CONV_W_ROW = 4


def _vec(vec_ref, name):
    r, c0, width = _VEC_SLOTS[name]
    return vec_ref[r:r + 1, c0:c0 + width]


def _silu_of_half(h):
    return h * jnp.tanh(h) + h


def _seg_store(seg_ref, x_ref, col0, width, halo):
    for slab in range(width // LANES):
        lanes = slice(col0 + slab * LANES, col0 + (slab + 1) * LANES)
        for g in range(SEGS):
            base = g * SEG_PITCH
            if g == 0:
                before = halo[:, slab * LANES:(slab + 1) * LANES]
            else:
                before = x_ref[g * SEG_ROWS - CONV_HALO:g * SEG_ROWS, lanes]
            seg_ref[slab, base:base + CONV_HALO, :] = before
            seg_ref[slab, base + CONV_HALO:base + SEG_PITCH, :] = x_ref[g * SEG_ROWS:(g + 1) * SEG_ROWS, lanes]


def _seg_row(seg_ref, r, slabs):
    return jnp.concatenate([seg_ref[slab, pl.ds(r, SEGS, stride=SEG_PITCH), :] for slab in slabs], axis=1)


def _segment_scan(a, b, h0, seg_ref, slabs):
    width = a.shape[-1]
    a3 = a.reshape(SEG_ROWS, SEGS, width)
    b3 = b.reshape(SEG_ROWS, SEGS, width)
    acum, hloc = [a3[0]], [b3[0]]
    for j in range(1, SEG_ROWS):
        acum.append(acum[-1] * a3[j])
        hloc.append(a3[j] * hloc[-1] + b3[j])
    end_a, end_h = acum[-1], hloc[-1]
    sub = lax.broadcasted_iota(jnp.int32, (SEGS, width), 0)
    carry = h0
    for _ in range(SEGS - 1):
        carry = jnp.where(sub == 0, h0, pltpu.roll(end_h + end_a * carry, 1, axis=0))
    for j in range(SEG_ROWS):
        hj = hloc[j] + acum[j] * carry
        for n, slab in enumerate(slabs):
            seg_ref[slab, pl.ds(j, SEGS, stride=SEG_PITCH), :] = hj[:, n * LANES:(n + 1) * LANES]
    last = end_h + end_a * carry
    return jnp.broadcast_to(last[SEGS - 1:SEGS, :], (SEGS, width))


def _seg_load_time_order(seg_ref, slabs):
    return jnp.concatenate(
        [jnp.concatenate([seg_ref[slab, g * SEG_PITCH:g * SEG_PITCH + SEG_ROWS, :] for g in range(SEGS)], axis=0)
         for slab in slabs], axis=1)


def _mix_pieces(proj_ref, y_ref, cm, p, s):
    first = cm == 0
    vec = p["vec"]

    for name in ("xb_ext", "s2_ext", "s4_ext", "s8_ext"):
        ref = s[name]
        ref[0:POOL_HALO, :] = jnp.where(first, 0.0, ref[0:POOL_HALO, :])
    cx_halo = jnp.where(first, 0.0, s["cx_halo"][...])
    hc = jnp.where(first, 0.0, s["h_carry"][...])

    v = _rmsnorm(proj_ref[:, A_WIDTH:2 * A_WIDTH], _vec(vec, "sgu_g")).astype(_BF16)
    ci = lax.broadcasted_iota(jnp.int32, (SGU_BLOCK, SGU_BLOCK), 0) // CHUNK
    cj = lax.broadcasted_iota(jnp.int32, (SGU_BLOCK, SGU_BLOCK), 1) // CHUNK
    mask = (cj <= ci).astype(_F32)
    wm = jnp.concatenate([(p["sgu_w"][h] * mask).astype(_BF16) for h in range(A_HEADS)], axis=1)
    head_of_lane = lax.broadcasted_iota(jnp.int32, (1, A_WIDTH), 1) // A_HEAD_DIM
    ys = []
    for n in range(TOK // SGU_BLOCK):
        vn = v[n * SGU_BLOCK:(n + 1) * SGU_BLOCK, :]
        rhs = jnp.concatenate([jnp.where(head_of_lane == h, vn, jnp.zeros_like(vn)) for h in range(A_HEADS)], axis=0)
        ys.append(jnp.dot(wm, rhs, preferred_element_type=_F32) + p["sgu_bias"][...])
    ya = proj_ref[:, 0:A_WIDTH] * jnp.concatenate(ys, axis=0)
    yield
    ya = _rms_scale(ya) * _silu_of_half(proj_ref[:, 2 * A_WIDTH:3 * A_WIDTH])
    y_ref[:, 0:A_WIDTH] = ya.astype(_BF16)
    yield

    xb_ext, s2_ext, s4_ext, s8_ext = s["xb_ext"], s["s2_ext"], s["s4_ext"], s["s8_ext"]
    bx = proj_ref[:, OFF_B:OFF_B + B_WIDTH]
    xb_ext[POOL_HALO:POOL_HALO + TOK, :] = bx
    s2 = bx + xb_ext[pl.ds(POOL_HALO - 1, TOK), :]
    s2_ext[POOL_HALO:POOL_HALO + TOK, :] = s2
    s4 = s2 + s2_ext[pl.ds(POOL_HALO - 2, TOK), :]
    s4_hi = s4[:, LANES:]
    s4_ext[POOL_HALO:POOL_HALO + TOK, :] = s4_hi
    s8 = s4_hi + s4_ext[pl.ds(POOL_HALO - 4, TOK), :]
    s8_ext[POOL_HALO:POOL_HALO + TOK, :] = s8
    s16 = s8 + s8_ext[pl.ds(POOL_HALO - 8, TOK), :]
    lane = lax.broadcasted_iota(jnp.int32, (1, LANES), 1)
    wsum = jnp.concatenate([jnp.where(lane < B_GROUP_DIM, s2[:, :LANES], s4[:, :LANES]),
                            jnp.where(lane < B_GROUP_DIM, s8, s16)], axis=1)
    lane_b = lax.broadcasted_iota(jnp.int32, (1, B_WIDTH), 1) // B_GROUP_DIM
    win = jnp.where(lane_b == 0, POOL_WINDOWS[0],
                    jnp.where(lane_b == 1, POOL_WINDOWS[1],
                              jnp.where(lane_b == 2, POOL_WINDOWS[2], POOL_WINDOWS[3])))
    inv_win = 1.0 / win.astype(_F32)
    t_head = cm * TOK + lax.broadcasted_iota(jnp.int32, (POOL_HALO, B_WIDTH), 0)
    cnt_head = jnp.minimum(t_head + 1, win).astype(_F32)
    mean = jnp.concatenate([wsum[0:POOL_HALO] / cnt_head, wsum[POOL_HALO:] * inv_win], axis=0)
    pooled = (mean - bx).astype(_BF16)
    for ref in (xb_ext, s2_ext, s4_ext, s8_ext):
        ref[0:POOL_HALO, :] = ref[TOK:TOK + POOL_HALO, :]
    yield
    yb = jnp.dot(pooled, p["pool_w"][...], preferred_element_type=_F32) + _vec(vec, "pool_b")
    yb = _rms_scale(yb * _vec(vec, "pool_scale"))
    yb = yb * _silu_of_half(proj_ref[:, OFF_B + B_WIDTH:OFF_B + 2 * B_WIDTH])
    y_ref[:, A_WIDTH:A_WIDTH + B_WIDTH] = yb.astype(_BF16)
    yield

    cw = vec[VEC_CONV_W_ROW:VEC_CONV_W_ROW + CONV_WIDTH, 0:C_WIDTH]
    half_neg_c_softplus = (-0.5 * LRU_C) * jax.nn.softplus(-_vec(vec, "lam"))
    seg_in, seg_out = s["seg_in"], s["seg_out"]
    all_slabs = tuple(range(C_WIDTH // LANES))
    _seg_store(seg_in, proj_ref, OFF_C, C_WIDTH, cx_halo)
    s["cx_halo"][...] = proj_ref[TOK - CONV_HALO:TOK, OFF_C:OFF_C + C_WIDTH]
    taps = [jnp.broadcast_to(cw[k:k + 1, :], (SEGS, C_WIDTH)) for k in range(CONV_WIDTH)]
    cbias = jnp.broadcast_to(_vec(vec, "conv_b"), (SEGS, C_WIDTH))
    xrow = {r: _seg_row(seg_in, r, all_slabs) for r in range(CONV_HALO - (CONV_WIDTH - 1), SEG_PITCH)}
    conv_rows = []
    for j in range(SEG_ROWS):
        acc = cbias + xrow[j + CONV_HALO] * taps[CONV_WIDTH - 1]
        for k in range(CONV_WIDTH - 1):
            acc = acc + xrow[j + CONV_HALO - (CONV_WIDTH - 1 - k)] * taps[k]
        conv_rows.append(acc)
    conv_all = jnp.concatenate(conv_rows, axis=0)
    yield
    for d in range(2):
        lo = d * C_HALF
        conv = conv_all[:, lo:lo + C_HALF]
        half_pre = jnp.dot(conv.astype(_BF16), p["gate_w"][d], preferred_element_type=_F32)
        t_r = jnp.tanh(half_pre[:, 0:C_HALF] + _vec(vec, "half_b_a")[:, lo:lo + C_HALF])
        t_i = jnp.tanh(half_pre[:, C_HALF:] + _vec(vec, "half_b_x")[:, lo:lo + C_HALF])
        half_nsp = half_neg_c_softplus[:, lo:lo + C_HALF]
        log_a = half_nsp * t_r + half_nsp
        a = jnp.exp(log_a)
        msq = -jnp.tanh(log_a) * (a * a + 1.0)
        mult = jnp.where(msq > 0.0, msq * lax.rsqrt(msq), 0.0)
        half_conv = 0.5 * conv
        bterm = mult * (half_conv * t_i + half_conv)
        yield
        slabs = all_slabs[d * (C_HALF // LANES):(d + 1) * (C_HALF // LANES)]
        s["h_carry"][:, lo:lo + C_HALF] = _segment_scan(a, bterm, hc[:, lo:lo + C_HALF], seg_out, slabs)
        yield
    yc = _rms_scale(_seg_load_time_order(seg_out, all_slabs))
    yc = yc * _silu_of_half(proj_ref[:, OFF_C + C_WIDTH:OFF_C + 2 * C_WIDTH])
    y_ref[:, A_WIDTH + B_WIDTH:] = yc.astype(_BF16)
    yield


def _matmul_tiles(lhs_ref, w_tile_refs, emit):
    def tile(ti):
        cols = slice(ti * MXU_COLS, (ti + 1) * MXU_COLS)
        emit(cols, jnp.dot(lhs_ref[...], w_tile_refs[ti][...], preferred_element_type=_F32))
    return [functools.partial(tile, ti) for ti in range(len(w_tile_refs))]


_PARAM_NAMES = ("vec", "sgu_w", "sgu_bias", "pool_w", "gate_w")
N_IN_TILES = D_IN // MXU_COLS
N_OUT_TILES = D_MODEL // MXU_COLS
_STATE_NAMES = ("xb_ext", "s2_ext", "s4_ext", "s8_ext", "cx_halo", "h_carry", "seg_in", "seg_out")


def _layer_kernel(*refs, is_last, chunks_per_seq):
    x_ref, xlag_ref = refs[0], refs[1]
    n = 2 + len(_PARAM_NAMES)
    p = dict(zip(_PARAM_NAMES, refs[2:n]))
    w_in_tiles = refs[n:n + N_IN_TILES]
    w_out_tiles = refs[n + N_IN_TILES:n + N_IN_TILES + N_OUT_TILES]
    n += N_IN_TILES + N_OUT_TILES
    o_ref = refs[n]
    scratch = refs[n + 1:]
    proj_bufs = scratch[0:PROJ_SLOTS]
    y_bufs = scratch[PROJ_SLOTS:PROJ_SLOTS + STEP_CHUNKS]
    hn_buf = scratch[PROJ_SLOTS + STEP_CHUNKS]
    s = dict(zip(_STATE_NAMES, scratch[PROJ_SLOTS + STEP_CHUNKS + 1:]))
    k = pl.program_id(0)

    @pl.when(k == 0)
    def _init():
        for ref in scratch:
            ref[...] = jnp.zeros(ref.shape, ref.dtype)

    for j in range(STEP_CHUNKS):
        rows = slice(j * TOK, (j + 1) * TOK)
        proj_ref = proj_bufs[j % PROJ_SLOTS]

        def emit_out(cols, t, rows=rows):
            o_ref[rows, cols] = xlag_ref[rows, cols] + t

        mxu_work = _matmul_tiles(y_bufs[j], w_out_tiles, emit_out)

        def emit_proj(cols, t, proj_ref=proj_ref):
            proj_ref[:, cols] = t

        mxu_work += _matmul_tiles(hn_buf, w_in_tiles, emit_proj)

        cm = lax.rem(STEP_CHUNKS * k + j + chunks_per_seq - 1, chunks_per_seq)
        mix = _mix_pieces(proj_bufs[(j - 1) % PROJ_SLOTS], y_bufs[(j - 1) % STEP_CHUNKS], cm, p, s)

        for t, thunk in enumerate(mxu_work):
            if t == 0:
                hn_buf[...] = _rms_scale(x_ref[rows, :]).astype(_BF16)
            else:
                next(mix, None)
            thunk()
            if is_last and t == N_OUT_TILES - 1:
                o_ref[rows, :] = _rmsnorm(o_ref[rows, :], _vec(p["vec"], "final_g"))
        for _ in mix:
            pass


def _block_diag(w):
    n, d = w.shape[-3], w.shape[-2]
    eye = jnp.eye(n, dtype=w.dtype)
    out = eye[:, None, :, None] * w[..., :, :, None, :]
    return out.reshape(w.shape[:-3] + (n * d, n * d))


def _prepare_params(norm_g, w_in, sgu_norm_g, sgu_w, sgu_b, pool_w, pool_b, pool_scale, conv_w, conv_b,
                    lru_wa, lru_ba, lru_wx, lru_bx, lru_lambda, branch_norm_g, w_out, final_g):
    depth = norm_g.shape[0]
    f32 = lambda a: a.astype(_F32)
    heads_per_half = C_HEADS // 2
    gate_w = jnp.stack([
        jnp.concatenate([_block_diag(lru_wa[:, d * heads_per_half:(d + 1) * heads_per_half]),
                         _block_diag(lru_wx[:, d * heads_per_half:(d + 1) * heads_per_half])], axis=-1)
        for d in range(2)], axis=1)
    half_gate_cols = jnp.concatenate([
        jnp.ones((2 * A_WIDTH,)), jnp.full((A_WIDTH,), 0.5), jnp.ones((B_WIDTH,)), jnp.full((B_WIDTH,), 0.5),
        jnp.ones((C_WIDTH,)), jnp.full((C_WIDTH,), 0.5)]).astype(_F32)
    sgu_bias = jnp.repeat(jnp.swapaxes(sgu_b, 1, 2), A_HEAD_DIM, axis=2).astype(_F32)

    def vec_row(*parts):
        row = jnp.concatenate([f32(a) for a in parts], axis=-1)
        return jnp.pad(row, ((0, 0), (0, D_MODEL - row.shape[-1])))

    vec = jnp.stack(
        [vec_row(jnp.broadcast_to(final_g, (depth, D_MODEL))),
         vec_row(sgu_norm_g, pool_b, pool_scale),
         vec_row(conv_b, lru_lambda),
         vec_row(0.5 * lru_ba, 0.5 * lru_bx)]
        + [vec_row(conv_w[:, k]) for k in range(CONV_WIDTH)], axis=1)
    small = dict(vec=vec, sgu_w=f32(sgu_w), sgu_bias=sgu_bias, pool_w=_block_diag(pool_w).astype(_BF16),
                 gate_w=(0.5 * gate_w).astype(_BF16))
    w_in_bf = (f32(w_in) * f32(norm_g)[:, :, None] * half_gate_cols).astype(_BF16)
    w_out_bf = (f32(w_out) * f32(branch_norm_g)[:, :, None]).astype(_BF16)
    return small, w_in_bf, w_out_bf


def _layer(x, small, w_in_bf, w_out_bf, layer, *, is_last, name):
    bsz, seq, _ = x.shape
    step_rows = STEP_CHUNKS * TOK
    assert seq % step_rows == 0 and TOK % SGU_BLOCK == 0 and STEP_CHUNKS % PROJ_SLOTS == 0
    steps_per_seq = seq // step_rows
    n_real = bsz * steps_per_seq
    small_ops = tuple(small[n] for n in _PARAM_NAMES)
    operands = (x, x) + small_ops + (w_in_bf,) * N_IN_TILES + (w_out_bf,) * N_OUT_TILES

    def x_map(k):
        kk = jnp.minimum(k, n_real - 1)
        return (kk // steps_per_seq, kk % steps_per_seq, 0)

    def lag_map(k):
        kl = jnp.maximum(k - LAG_STEPS, 0)
        return (kl // steps_per_seq, kl % steps_per_seq, 0)

    def of_layer(shape):
        return pl.BlockSpec((None,) + tuple(shape[1:]), lambda k: (layer,) + (0,) * (len(shape) - 1),
                            pipeline_mode=pl.Buffered(1))

    in_specs = [pl.BlockSpec((None, step_rows, D_MODEL), x_map),
                pl.BlockSpec((None, step_rows, D_MODEL), lag_map)]
    in_specs += [of_layer(op.shape) for op in small_ops]
    for w, tiles in ((w_in_bf, N_IN_TILES), (w_out_bf, N_OUT_TILES)):
        in_specs += [pl.BlockSpec((None, w.shape[1], MXU_COLS), functools.partial(lambda k, ti: (layer, 0, ti), ti=ti),
                                  pipeline_mode=pl.Buffered(1)) for ti in range(tiles)]
    scratch_shapes = (
        [pltpu.VMEM((TOK, D_IN), _F32) for _ in range(PROJ_SLOTS)]
        + [pltpu.VMEM((TOK, D_MODEL), _BF16) for _ in range(STEP_CHUNKS)]
        + [pltpu.VMEM((TOK, D_MODEL), _BF16),
           pltpu.VMEM((TOK + POOL_HALO, B_WIDTH), _F32),
           pltpu.VMEM((TOK + POOL_HALO, B_WIDTH), _F32),
           pltpu.VMEM((TOK + POOL_HALO, LANES), _F32),
           pltpu.VMEM((TOK + POOL_HALO, LANES), _F32),
           pltpu.VMEM((CONV_HALO, C_WIDTH), _F32),
           pltpu.VMEM((SUBLANES, C_WIDTH), _F32),
           pltpu.VMEM((C_WIDTH // LANES, SEGS * SEG_PITCH, LANES), _F32),
           pltpu.VMEM((C_WIDTH // LANES, SEGS * SEG_PITCH, LANES), _F32)])
    return pl.pallas_call(
        functools.partial(_layer_kernel, is_last=is_last, chunks_per_seq=seq // TOK),
        out_shape=jax.ShapeDtypeStruct(x.shape, x.dtype),
        grid=(n_real + LAG_STEPS,),
        in_specs=in_specs,
        out_specs=pl.BlockSpec((None, step_rows, D_MODEL), lag_map),
        scratch_shapes=scratch_shapes,
        compiler_params=pltpu.CompilerParams(
            dimension_semantics=("arbitrary",),
            vmem_limit_bytes=VMEM_LIMIT_BYTES,
        ),
        name=name,
    )(*operands)


def kernel(x, norm_g, w_in, sgu_norm_g, sgu_w, sgu_b, pool_w, pool_b, pool_scale, conv_w, conv_b,
           lru_wa, lru_ba, lru_wx, lru_bx, lru_lambda, branch_norm_g, w_out, final_g):
    small, w_in_bf, w_out_bf = _prepare_params(
        norm_g, w_in, sgu_norm_g, sgu_w, sgu_b, pool_w, pool_b, pool_scale, conv_w, conv_b,
        lru_wa, lru_ba, lru_wx, lru_bx, lru_lambda, branch_norm_g, w_out, final_g)
    for l in range(DEPTH):
        x = _layer(x, small, w_in_bf, w_out_bf, l, is_last=(l == DEPTH - 1), name=f"hybrid_layer{l}")
    return x
```

```python
import functools

import jax
import jax.numpy as jnp
from jax import lax
from jax.experimental import pallas as pl
from jax.experimental.pallas import tpu as pltpu

D_MODEL = 1024
DEPTH = 2
CHUNK = 64
EPS = 1e-6
A_WIDTH = 256
A_HEADS = 4
A_HEAD_DIM = 64
SGU_BLOCK = 128
B_WIDTH = 256
POOL_WINDOWS = (2, 4, 8, 16)
B_GROUP_DIM = 64
C_WIDTH = 512
C_HEADS = 8
C_HEAD_DIM = 64
CONV_WIDTH = 4
LRU_C = 8.0
D_IN = 3 * A_WIDTH + 2 * B_WIDTH + 2 * C_WIDTH
OFF_B = 3 * A_WIDTH
OFF_C = OFF_B + 2 * B_WIDTH

SUBLANES = 8
LANES = 128
MXU_COLS = 256
TOK = 256
STEP_CHUNKS = 4
PROJ_SLOTS = 2
LAG_STEPS = 1
POOL_HALO = 16
CONV_HALO = 8
C_HALF = C_WIDTH // 2
SEGS = SUBLANES
SEG_ROWS = TOK // SEGS
SEG_PITCH = SEG_ROWS + CONV_HALO
VMEM_LIMIT_BYTES = 56 * 1024 * 1024

_BF16 = jnp.bfloat16
_F32 = jnp.float32
F32_MAX = float(jnp.finfo(jnp.float32).max)


def _rms_scale(x):
    ms = jnp.mean(x * x, axis=-1, keepdims=True)
    return x * lax.rsqrt(ms + EPS)


def _rmsnorm(x, g):
    return _rms_scale(x) * g


VEC_ROWS = 8
_VEC_SLOTS = {
    "final_g": (0, 0, D_MODEL),
    "sgu_g": (1, 0, A_WIDTH), "pool_b": (1, A_WIDTH, B_WIDTH), "pool_scale": (1, A_WIDTH + B_WIDTH, B_WIDTH),
    "half_conv_b": (2, 0, C_WIDTH), "lam": (2, C_WIDTH, C_WIDTH),
    "half_b_a": (3, 0, C_WIDTH), "half_b_x": (3, C_WIDTH, C_WIDTH),
}
VEC_CONV_W_ROW = 4


def _vec(vec_ref, name):
    r, c0, width = _VEC_SLOTS[name]
    return vec_ref[r:r + 1, c0:c0 + width]


def _silu_of_half(h):
    return h * jnp.tanh(h) + h


def _seg_store(seg_ref, x_ref, col0, width, halo):
    for slab in range(width // LANES):
        lanes = slice(col0 + slab * LANES, col0 + (slab + 1) * LANES)
        for g in range(SEGS):
            base = g * SEG_PITCH
            if g == 0:
                before = halo[:, slab * LANES:(slab + 1) * LANES]
            else:
                before = x_ref[g * SEG_ROWS - CONV_HALO:g * SEG_ROWS, lanes]
            seg_ref[slab, base:base + CONV_HALO, :] = before
            seg_ref[slab, base + CONV_HALO:base + SEG_PITCH, :] = x_ref[g * SEG_ROWS:(g + 1) * SEG_ROWS, lanes]


def _seg_row(seg_ref, r, slabs):
    return jnp.concatenate([seg_ref[slab, pl.ds(r, SEGS, stride=SEG_PITCH), :] for slab in slabs], axis=1)


def _segment_scan(a, b, h0, seg_ref, slabs):
    width = a.shape[-1]
    a3 = a.reshape(SEG_ROWS, SEGS, width)
    b3 = b.reshape(SEG_ROWS, SEGS, width)
    acum, hloc = [a3[0]], [b3[0]]
    for j in range(1, SEG_ROWS):
        acum.append(acum[-1] * a3[j])
        hloc.append(a3[j] * hloc[-1] + b3[j])
    end_a, end_h = acum[-1], hloc[-1]
    sub = lax.broadcasted_iota(jnp.int32, (SEGS, width), 0)
    carry = h0
    for _ in range(SEGS - 1):
        carry = jnp.where(sub == 0, h0, pltpu.roll(end_h + end_a * carry, 1, axis=0))
    for j in range(SEG_ROWS):
        hj = hloc[j] + acum[j] * carry
        for n, slab in enumerate(slabs):
            seg_ref[slab, pl.ds(j, SEGS, stride=SEG_PITCH), :] = hj[:, n * LANES:(n + 1) * LANES]
    last = end_h + end_a * carry
    return jnp.broadcast_to(last[SEGS - 1:SEGS, :], (SEGS, width))


def _seg_load_time_order(seg_ref, slabs):
    return jnp.concatenate(
        [jnp.concatenate([seg_ref[slab, g * SEG_PITCH:g * SEG_PITCH + SEG_ROWS, :] for g in range(SEGS)], axis=0)
         for slab in slabs], axis=1)


def _mix_pieces(proj_ref, y_ref, cm, p, s):
    first = cm == 0
    vec = p["vec"]

    for name in ("xb_ext", "s2_ext", "s4_ext", "s8_ext"):
        ref = s[name]
        ref[0:POOL_HALO, :] = jnp.where(first, 0.0, ref[0:POOL_HALO, :])
    cx_halo = jnp.where(first, 0.0, s["cx_halo"][...])
    hc = jnp.where(first, 0.0, s["h_carry"][...])

    v = _rmsnorm(proj_ref[:, A_WIDTH:2 * A_WIDTH], _vec(vec, "sgu_g")).astype(_BF16)
    ci = lax.broadcasted_iota(jnp.int32, (SGU_BLOCK, SGU_BLOCK), 0) // CHUNK
    cj = lax.broadcasted_iota(jnp.int32, (SGU_BLOCK, SGU_BLOCK), 1) // CHUNK
    mask = (cj <= ci).astype(_F32)
    wm = jnp.concatenate([(p["sgu_w"][h] * mask).astype(_BF16) for h in range(A_HEADS)], axis=1)
    head_of_lane = lax.broadcasted_iota(jnp.int32, (1, A_WIDTH), 1) // A_HEAD_DIM
    ys = []
    for n in range(TOK // SGU_BLOCK):
        vn = v[n * SGU_BLOCK:(n + 1) * SGU_BLOCK, :]
        rhs = jnp.concatenate([jnp.where(head_of_lane == h, vn, jnp.zeros_like(vn)) for h in range(A_HEADS)], axis=0)
        ys.append(jnp.dot(wm, rhs, preferred_element_type=_F32) + p["sgu_bias"][...])
    ya = proj_ref[:, 0:A_WIDTH] * jnp.concatenate(ys, axis=0)
    yield
    ya = _rms_scale(ya) * _silu_of_half(proj_ref[:, 2 * A_WIDTH:3 * A_WIDTH])
    y_ref[:, 0:A_WIDTH] = ya.astype(_BF16)
    yield

    xb_ext, s2_ext, s4_ext, s8_ext = s["xb_ext"], s["s2_ext"], s["s4_ext"], s["s8_ext"]
    bx = proj_ref[:, OFF_B:OFF_B + B_WIDTH]
    xb_ext[POOL_HALO:POOL_HALO + TOK, :] = bx
    s2 = bx + xb_ext[pl.ds(POOL_HALO - 1, TOK), :]
    s2_ext[POOL_HALO:POOL_HALO + TOK, :] = s2
    s4 = s2 + s2_ext[pl.ds(POOL_HALO - 2, TOK), :]
    s4_hi = s4[:, LANES:]
    s4_ext[POOL_HALO:POOL_HALO + TOK, :] = s4_hi
    s8 = s4_hi + s4_ext[pl.ds(POOL_HALO - 4, TOK), :]
    s8_ext[POOL_HALO:POOL_HALO + TOK, :] = s8
    s16 = s8 + s8_ext[pl.ds(POOL_HALO - 8, TOK), :]
    lane = lax.broadcasted_iota(jnp.int32, (1, LANES), 1)
    wsum = jnp.concatenate([jnp.where(lane < B_GROUP_DIM, s2[:, :LANES], s4[:, :LANES]),
                            jnp.where(lane < B_GROUP_DIM, s8, s16)], axis=1)
    lane_b = lax.broadcasted_iota(jnp.int32, (1, B_WIDTH), 1) // B_GROUP_DIM
    win = jnp.where(lane_b == 0, POOL_WINDOWS[0],
                    jnp.where(lane_b == 1, POOL_WINDOWS[1],
                              jnp.where(lane_b == 2, POOL_WINDOWS[2], POOL_WINDOWS[3])))
    inv_win = 1.0 / win.astype(_F32)
    t_head = cm * TOK + lax.broadcasted_iota(jnp.int32, (POOL_HALO, B_WIDTH), 0)
    cnt_head = jnp.minimum(t_head + 1, win).astype(_F32)
    mean = jnp.concatenate([wsum[0:POOL_HALO] / cnt_head, wsum[POOL_HALO:] * inv_win], axis=0)
    pooled = (mean - bx).astype(_BF16)
    for ref in (xb_ext, s2_ext, s4_ext, s8_ext):
        ref[0:POOL_HALO, :] = ref[TOK:TOK + POOL_HALO, :]
    yield
    yb = jnp.dot(pooled, p["pool_w"][...], preferred_element_type=_F32) + _vec(vec, "pool_b")
    yb = _rms_scale(yb * _vec(vec, "pool_scale"))
    yb = yb * _silu_of_half(proj_ref[:, OFF_B + B_WIDTH:OFF_B + 2 * B_WIDTH])
    y_ref[:, A_WIDTH:A_WIDTH + B_WIDTH] = yb.astype(_BF16)
    yield

    cw = vec[VEC_CONV_W_ROW:VEC_CONV_W_ROW + CONV_WIDTH, 0:C_WIDTH]
    half_neg_c_softplus = (-0.5 * LRU_C) * jax.nn.softplus(-_vec(vec, "lam"))
    seg_in, seg_out = s["seg_in"], s["seg_out"]
    all_slabs = tuple(range(C_WIDTH // LANES))
    _seg_store(seg_in, proj_ref, OFF_C, C_WIDTH, cx_halo)
    s["cx_halo"][...] = proj_ref[TOK - CONV_HALO:TOK, OFF_C:OFF_C + C_WIDTH]
    taps = [jnp.broadcast_to(cw[k:k + 1, :], (SEGS, C_WIDTH)) for k in range(CONV_WIDTH)]
    cbias = jnp.broadcast_to(_vec(vec, "half_conv_b"), (SEGS, C_WIDTH))
    xrow = {r: _seg_row(seg_in, r, all_slabs) for r in range(CONV_HALO - (CONV_WIDTH - 1), SEG_PITCH)}
    conv_rows = []
    for j in range(SEG_ROWS):
        acc = cbias + xrow[j + CONV_HALO] * taps[CONV_WIDTH - 1]
        for k in range(CONV_WIDTH - 1):
            acc = acc + xrow[j + CONV_HALO - (CONV_WIDTH - 1 - k)] * taps[k]
        conv_rows.append(acc)
    half_conv_all = jnp.concatenate(conv_rows, axis=0)
    yield
    for d in range(2):
        lo = d * C_HALF
        half_conv = half_conv_all[:, lo:lo + C_HALF]
        half_pre = jnp.dot(half_conv.astype(_BF16), p["gate_w"][d], preferred_element_type=_F32)
        t_r = jnp.tanh(half_pre[:, 0:C_HALF] + _vec(vec, "half_b_a")[:, lo:lo + C_HALF])
        t_i = jnp.tanh(half_pre[:, C_HALF:] + _vec(vec, "half_b_x")[:, lo:lo + C_HALF])
        half_nsp = half_neg_c_softplus[:, lo:lo + C_HALF]
        log_a = half_nsp * t_r + half_nsp
        a = jnp.exp(log_a)
        msq = jnp.tanh(log_a) * (-1.0 - a * a)
        mult = msq * jnp.minimum(lax.rsqrt(msq), F32_MAX)
        bterm = mult * (half_conv * t_i + half_conv)
        yield
        slabs = all_slabs[d * (C_HALF // LANES):(d + 1) * (C_HALF // LANES)]
        s["h_carry"][:, lo:lo + C_HALF] = _segment_scan(a, bterm, hc[:, lo:lo + C_HALF], seg_out, slabs)
        yield
    yc = _rms_scale(_seg_load_time_order(seg_out, all_slabs))
    yc = yc * _silu_of_half(proj_ref[:, OFF_C + C_WIDTH:OFF_C + 2 * C_WIDTH])
    y_ref[:, A_WIDTH + B_WIDTH:] = yc.astype(_BF16)
    yield


def _matmul_tiles(lhs_ref, w_tile_refs, emit):
    def tile(ti):
        cols = slice(ti * MXU_COLS, (ti + 1) * MXU_COLS)
        emit(cols, jnp.dot(lhs_ref[...], w_tile_refs[ti][...], preferred_element_type=_F32))
    return [functools.partial(tile, ti) for ti in range(len(w_tile_refs))]


_PARAM_NAMES = ("vec", "sgu_w", "sgu_bias", "pool_w", "gate_w")
N_IN_TILES = D_IN // MXU_COLS
N_OUT_TILES = D_MODEL // MXU_COLS
_STATE_NAMES = ("xb_ext", "s2_ext", "s4_ext", "s8_ext", "cx_halo", "h_carry", "seg_in", "seg_out")


def _layer_kernel(*refs, is_last, chunks_per_seq):
    x_ref, xlag_ref = refs[0], refs[1]
    n = 2 + len(_PARAM_NAMES)
    p = dict(zip(_PARAM_NAMES, refs[2:n]))
    w_in_tiles = refs[n:n + N_IN_TILES]
    w_out_tiles = refs[n + N_IN_TILES:n + N_IN_TILES + N_OUT_TILES]
    n += N_IN_TILES + N_OUT_TILES
    o_ref = refs[n]
    scratch = refs[n + 1:]
    proj_bufs = scratch[0:PROJ_SLOTS]
    y_bufs = scratch[PROJ_SLOTS:PROJ_SLOTS + STEP_CHUNKS]
    hn_buf = scratch[PROJ_SLOTS + STEP_CHUNKS]
    s = dict(zip(_STATE_NAMES, scratch[PROJ_SLOTS + STEP_CHUNKS + 1:]))
    k = pl.program_id(0)

    @pl.when(k == 0)
    def _init():
        for ref in scratch:
            ref[...] = jnp.zeros(ref.shape, ref.dtype)

    for j in range(STEP_CHUNKS):
        rows = slice(j * TOK, (j + 1) * TOK)
        proj_ref = proj_bufs[j % PROJ_SLOTS]

        def emit_out(cols, t, rows=rows):
            o_ref[rows, cols] = xlag_ref[rows, cols] + t

        mxu_work = _matmul_tiles(y_bufs[j], w_out_tiles, emit_out)

        def emit_proj(cols, t, proj_ref=proj_ref):
            proj_ref[:, cols] = t

        mxu_work += _matmul_tiles(hn_buf, w_in_tiles, emit_proj)

        cm = lax.rem(STEP_CHUNKS * k + j + chunks_per_seq - 1, chunks_per_seq)
        mix = _mix_pieces(proj_bufs[(j - 1) % PROJ_SLOTS], y_bufs[(j - 1) % STEP_CHUNKS], cm, p, s)

        for t, thunk in enumerate(mxu_work):
            if t == 0:
                hn_buf[...] = _rms_scale(x_ref[rows, :]).astype(_BF16)
            else:
                next(mix, None)
            thunk()
            if is_last and t == N_OUT_TILES - 1:
                o_ref[rows, :] = _rmsnorm(o_ref[rows, :], _vec(p["vec"], "final_g"))
        for _ in mix:
            pass


def _block_diag(w):
    n, d = w.shape[-3], w.shape[-2]
    eye = jnp.eye(n, dtype=w.dtype)
    out = eye[:, None, :, None] * w[..., :, :, None, :]
    return out.reshape(w.shape[:-3] + (n * d, n * d))


def _prepare_params(norm_g, w_in, sgu_norm_g, sgu_w, sgu_b, pool_w, pool_b, pool_scale, conv_w, conv_b,
                    lru_wa, lru_ba, lru_wx, lru_bx, lru_lambda, branch_norm_g, w_out, final_g):
    depth = norm_g.shape[0]
    f32 = lambda a: a.astype(_F32)
    heads_per_half = C_HEADS // 2
    gate_w = jnp.stack([
        jnp.concatenate([_block_diag(lru_wa[:, d * heads_per_half:(d + 1) * heads_per_half]),
                         _block_diag(lru_wx[:, d * heads_per_half:(d + 1) * heads_per_half])], axis=-1)
        for d in range(2)], axis=1)
    half_gate_cols = jnp.concatenate([
        jnp.ones((2 * A_WIDTH,)), jnp.full((A_WIDTH,), 0.5), jnp.ones((B_WIDTH,)), jnp.full((B_WIDTH,), 0.5),
        jnp.ones((C_WIDTH,)), jnp.full((C_WIDTH,), 0.5)]).astype(_F32)
    sgu_bias = jnp.repeat(jnp.swapaxes(sgu_b, 1, 2), A_HEAD_DIM, axis=2).astype(_F32)

    def vec_row(*parts):
        row = jnp.concatenate([f32(a) for a in parts], axis=-1)
        return jnp.pad(row, ((0, 0), (0, D_MODEL - row.shape[-1])))

    vec = jnp.stack(
        [vec_row(jnp.broadcast_to(final_g, (depth, D_MODEL))),
         vec_row(sgu_norm_g, pool_b, pool_scale),
         vec_row(0.5 * conv_b, lru_lambda),
         vec_row(0.5 * lru_ba, 0.5 * lru_bx)]
        + [vec_row(0.5 * conv_w[:, k]) for k in range(CONV_WIDTH)], axis=1)
    small = dict(vec=vec, sgu_w=f32(sgu_w), sgu_bias=sgu_bias, pool_w=_block_diag(pool_w).astype(_BF16),
                 gate_w=gate_w.astype(_BF16))
    w_in_bf = (f32(w_in) * f32(norm_g)[:, :, None] * half_gate_cols).astype(_BF16)
    w_out_bf = (f32(w_out) * f32(branch_norm_g)[:, :, None]).astype(_BF16)
    return small, w_in_bf, w_out_bf


def _layer(x, small, w_in_bf, w_out_bf, layer, *, is_last, name):
    bsz, seq, _ = x.shape
    step_rows = STEP_CHUNKS * TOK
    assert seq % step_rows == 0 and TOK % SGU_BLOCK == 0 and STEP_CHUNKS % PROJ_SLOTS == 0
    steps_per_seq = seq // step_rows
    n_real = bsz * steps_per_seq
    small_ops = tuple(small[n] for n in _PARAM_NAMES)
    operands = (x, x) + small_ops + (w_in_bf,) * N_IN_TILES + (w_out_bf,) * N_OUT_TILES

    def x_map(k):
        kk = jnp.minimum(k, n_real - 1)
        return (kk // steps_per_seq, kk % steps_per_seq, 0)

    def lag_map(k):
        kl = jnp.maximum(k - LAG_STEPS, 0)
        return (kl // steps_per_seq, kl % steps_per_seq, 0)

    def of_layer(shape):
        return pl.BlockSpec((None,) + tuple(shape[1:]), lambda k: (layer,) + (0,) * (len(shape) - 1),
                            pipeline_mode=pl.Buffered(1))

    in_specs = [pl.BlockSpec((None, step_rows, D_MODEL), x_map),
                pl.BlockSpec((None, step_rows, D_MODEL), lag_map)]
    in_specs += [of_layer(op.shape) for op in small_ops]
    for w, tiles in ((w_in_bf, N_IN_TILES), (w_out_bf, N_OUT_TILES)):
        in_specs += [pl.BlockSpec((None, w.shape[1], MXU_COLS), functools.partial(lambda k, ti: (layer, 0, ti), ti=ti),
                                  pipeline_mode=pl.Buffered(1)) for ti in range(tiles)]
    scratch_shapes = (
        [pltpu.VMEM((TOK, D_IN), _F32) for _ in range(PROJ_SLOTS)]
        + [pltpu.VMEM((TOK, D_MODEL), _BF16) for _ in range(STEP_CHUNKS)]
        + [pltpu.VMEM((TOK, D_MODEL), _BF16),
           pltpu.VMEM((TOK + POOL_HALO, B_WIDTH), _F32),
           pltpu.VMEM((TOK + POOL_HALO, B_WIDTH), _F32),
           pltpu.VMEM((TOK + POOL_HALO, LANES), _F32),
           pltpu.VMEM((TOK + POOL_HALO, LANES), _F32),
           pltpu.VMEM((CONV_HALO, C_WIDTH), _F32),
           pltpu.VMEM((SUBLANES, C_WIDTH), _F32),
           pltpu.VMEM((C_WIDTH // LANES, SEGS * SEG_PITCH, LANES), _F32),
           pltpu.VMEM((C_WIDTH // LANES, SEGS * SEG_PITCH, LANES), _F32)])
    return pl.pallas_call(
        functools.partial(_layer_kernel, is_last=is_last, chunks_per_seq=seq // TOK),
        out_shape=jax.ShapeDtypeStruct(x.shape, x.dtype),
        grid=(n_real + LAG_STEPS,),
        in_specs=in_specs,
        out_specs=pl.BlockSpec((None, step_rows, D_MODEL), lag_map),
        scratch_shapes=scratch_shapes,
        compiler_params=pltpu.CompilerParams(
            dimension_semantics=("arbitrary",),
            vmem_limit_bytes=VMEM_LIMIT_BYTES,
        ),
        name=name,
    )(*operands)


def kernel(x, norm_g, w_in, sgu_norm_g, sgu_w, sgu_b, pool_w, pool_b, pool_scale, conv_w, conv_b,
           lru_wa, lru_ba, lru_wx, lru_bx, lru_lambda, branch_norm_g, w_out, final_g):
    small, w_in_bf, w_out_bf = _prepare_params(
        norm_g, w_in, sgu_norm_g, sgu_w, sgu_b, pool_w, pool_b, pool_scale, conv_w, conv_b,
        lru_wa, lru_ba, lru_wx, lru_bx, lru_lambda, branch_norm_g, w_out, final_g)
    for l in range(DEPTH):
        x = _layer(x, small, w_in_bf, w_out_bf, l, is_last=(l == DEPTH - 1), name=f"hybrid_layer{l}")
    return x
```

```python
import functools

import jax
import jax.numpy as jnp
from jax import lax
from jax.experimental import pallas as pl
from jax.experimental.pallas import tpu as pltpu

D_MODEL = 1024
DEPTH = 2
CHUNK = 64
EPS = 1e-6
A_WIDTH = 256
A_HEADS = 4
A_HEAD_DIM = 64
SGU_BLOCK = 128
B_WIDTH = 256
POOL_WINDOWS = (2, 4, 8, 16)
B_GROUP_DIM = 64
C_WIDTH = 512
C_HEADS = 8
C_HEAD_DIM = 64
CONV_WIDTH = 4
LRU_C = 8.0
D_IN = 3 * A_WIDTH + 2 * B_WIDTH + 2 * C_WIDTH
OFF_B = 3 * A_WIDTH
OFF_C = OFF_B + 2 * B_WIDTH

SUBLANES = 8
LANES = 128
MXU_COLS = 256
TOK = 256
STEP_CHUNKS = 4
PROJ_SLOTS = 2
LAG_STEPS = 1
POOL_HALO = 16
CONV_HALO = 8
C_HALF = C_WIDTH // 2
SEGS = SUBLANES
SEG_ROWS = TOK // SEGS
SEG_PITCH = SEG_ROWS + CONV_HALO
VMEM_LIMIT_BYTES = 56 * 1024 * 1024

_BF16 = jnp.bfloat16
_F32 = jnp.float32
F32_MAX = float(jnp.finfo(jnp.float32).max)


def _rms_scale(x):
    ms = jnp.mean(x * x, axis=-1, keepdims=True)
    return x * lax.rsqrt(ms + EPS)


def _rmsnorm(x, g):
    return _rms_scale(x) * g


VEC_ROWS = 8
_VEC_SLOTS = {
    "final_g": (0, 0, D_MODEL),
    "sgu_g": (1, 0, A_WIDTH), "pool_b": (1, A_WIDTH, B_WIDTH), "pool_scale": (1, A_WIDTH + B_WIDTH, B_WIDTH),
    "half_conv_b": (2, 0, C_WIDTH), "lam": (2, C_WIDTH, C_WIDTH),
    "half_b_a": (3, 0, C_WIDTH), "half_b_x": (3, C_WIDTH, C_WIDTH),
}
VEC_CONV_W_ROW = 4


def _vec(vec_ref, name):
    r, c0, width = _VEC_SLOTS[name]
    return vec_ref[r:r + 1, c0:c0 + width]


def _silu_of_half(h):
    return h * jnp.tanh(h) + h


def _seg_store(seg_ref, x_ref, col0, width, halo):
    for slab in range(width // LANES):
        lanes = slice(col0 + slab * LANES, col0 + (slab + 1) * LANES)
        for g in range(SEGS):
            base = g * SEG_PITCH
            if g == 0:
                before = halo[:, slab * LANES:(slab + 1) * LANES]
            else:
                before = x_ref[g * SEG_ROWS - CONV_HALO:g * SEG_ROWS, lanes]
            seg_ref[slab, base:base + CONV_HALO, :] = before
            seg_ref[slab, base + CONV_HALO:base + SEG_PITCH, :] = x_ref[g * SEG_ROWS:(g + 1) * SEG_ROWS, lanes]


def _seg_row(seg_ref, r, slabs):
    return jnp.concatenate([seg_ref[slab, pl.ds(r, SEGS, stride=SEG_PITCH), :] for slab in slabs], axis=1)


def _segment_scan(a, b, h0, seg_ref, slabs):
    width = a.shape[-1]
    a3 = a.reshape(SEG_ROWS, SEGS, width)
    b3 = b.reshape(SEG_ROWS, SEGS, width)
    acum, hloc = [a3[0]], [b3[0]]
    for j in range(1, SEG_ROWS):
        acum.append(acum[-1] * a3[j])
        hloc.append(a3[j] * hloc[-1] + b3[j])
    end_a, end_h = acum[-1], hloc[-1]
    sub = lax.broadcasted_iota(jnp.int32, (SEGS, width), 0)
    carry = h0
    for _ in range(SEGS - 1):
        carry = jnp.where(sub == 0, h0, pltpu.roll(end_h + end_a * carry, 1, axis=0))
    for j in range(SEG_ROWS):
        hj = hloc[j] + acum[j] * carry
        for n, slab in enumerate(slabs):
            seg_ref[slab, pl.ds(j, SEGS, stride=SEG_PITCH), :] = hj[:, n * LANES:(n + 1) * LANES]
    last = end_h + end_a * carry
    return jnp.broadcast_to(last[SEGS - 1:SEGS, :], (SEGS, width))


def _seg_load_time_order(seg_ref, slabs):
    return jnp.concatenate(
        [jnp.concatenate([seg_ref[slab, g * SEG_PITCH:g * SEG_PITCH + SEG_ROWS, :] for g in range(SEGS)], axis=0)
         for slab in slabs], axis=1)


def _mix_pieces(proj_ref, y_ref, cm, p, s):
    first = cm == 0
    vec = p["vec"]

    for name in ("xb_ext", "s2_ext", "s4_ext", "s8_ext"):
        ref = s[name]
        ref[0:POOL_HALO, :] = jnp.where(first, 0.0, ref[0:POOL_HALO, :])
    cx_halo = jnp.where(first, 0.0, s["cx_halo"][...])
    hc = jnp.where(first, 0.0, s["h_carry"][...])

    v = _rmsnorm(proj_ref[:, A_WIDTH:2 * A_WIDTH], _vec(vec, "sgu_g")).astype(_BF16)
    ci = lax.broadcasted_iota(jnp.int32, (SGU_BLOCK, SGU_BLOCK), 0) // CHUNK
    cj = lax.broadcasted_iota(jnp.int32, (SGU_BLOCK, SGU_BLOCK), 1) // CHUNK
    mask = (cj <= ci).astype(_F32)
    wm = jnp.concatenate([(p["sgu_w"][h] * mask).astype(_BF16) for h in range(A_HEADS)], axis=1)
    head_of_lane = lax.broadcasted_iota(jnp.int32, (1, A_WIDTH), 1) // A_HEAD_DIM
    ys = []
    for n in range(TOK // SGU_BLOCK):
        vn = v[n * SGU_BLOCK:(n + 1) * SGU_BLOCK, :]
        rhs = jnp.concatenate([jnp.where(head_of_lane == h, vn, jnp.zeros_like(vn)) for h in range(A_HEADS)], axis=0)
        ys.append(jnp.dot(wm, rhs, preferred_element_type=_F32) + p["sgu_bias"][...])
    ya = proj_ref[:, 0:A_WIDTH] * jnp.concatenate(ys, axis=0)
    yield
    ya = _rms_scale(ya) * _silu_of_half(proj_ref[:, 2 * A_WIDTH:3 * A_WIDTH])
    y_ref[:, 0:A_WIDTH] = ya.astype(_BF16)
    yield

    xb_ext, s2_ext, s4_ext, s8_ext = s["xb_ext"], s["s2_ext"], s["s4_ext"], s["s8_ext"]
    bx = proj_ref[:, OFF_B:OFF_B + B_WIDTH]
    xb_ext[POOL_HALO:POOL_HALO + TOK, :] = bx
    s2 = bx + xb_ext[pl.ds(POOL_HALO - 1, TOK), :]
    s2_ext[POOL_HALO:POOL_HALO + TOK, :] = s2
    s4 = s2 + s2_ext[pl.ds(POOL_HALO - 2, TOK), :]
    s4_hi = s4[:, LANES:]
    s4_ext[POOL_HALO:POOL_HALO + TOK, :] = s4_hi
    s8 = s4_hi + s4_ext[pl.ds(POOL_HALO - 4, TOK), :]
    s8_ext[POOL_HALO:POOL_HALO + TOK, :] = s8
    s16 = s8 + s8_ext[pl.ds(POOL_HALO - 8, TOK), :]
    lane = lax.broadcasted_iota(jnp.int32, (1, LANES), 1)
    wsum = jnp.concatenate([jnp.where(lane < B_GROUP_DIM, s2[:, :LANES], s4[:, :LANES]),
                            jnp.where(lane < B_GROUP_DIM, s8, s16)], axis=1)
    lane_b = lax.broadcasted_iota(jnp.int32, (1, B_WIDTH), 1) // B_GROUP_DIM
    win = jnp.where(lane_b == 0, POOL_WINDOWS[0],
                    jnp.where(lane_b == 1, POOL_WINDOWS[1],
                              jnp.where(lane_b == 2, POOL_WINDOWS[2], POOL_WINDOWS[3])))
    inv_win = 1.0 / win.astype(_F32)
    t_head = cm * TOK + lax.broadcasted_iota(jnp.int32, (POOL_HALO, B_WIDTH), 0)
    cnt_head = jnp.minimum(t_head + 1, win).astype(_F32)
    mean = jnp.concatenate([wsum[0:POOL_HALO] / cnt_head, wsum[POOL_HALO:] * inv_win], axis=0)
    pooled = (mean - bx).astype(_BF16)
    for ref in (xb_ext, s2_ext, s4_ext, s8_ext):
        ref[0:POOL_HALO, :] = ref[TOK:TOK + POOL_HALO, :]
    yield
    yb = jnp.dot(pooled, p["pool_w"][...], preferred_element_type=_F32) + _vec(vec, "pool_b")
    yb = _rms_scale(yb * _vec(vec, "pool_scale"))
    yb = yb * _silu_of_half(proj_ref[:, OFF_B + B_WIDTH:OFF_B + 2 * B_WIDTH])
    y_ref[:, A_WIDTH:A_WIDTH + B_WIDTH] = yb.astype(_BF16)
    yield

    cw = vec[VEC_CONV_W_ROW:VEC_CONV_W_ROW + CONV_WIDTH, 0:C_WIDTH]
    half_neg_c_softplus = (-0.5 * LRU_C) * jax.nn.softplus(-_vec(vec, "lam"))
    seg_in, seg_out = s["seg_in"], s["seg_out"]
    all_slabs = tuple(range(C_WIDTH // LANES))
    _seg_store(seg_in, proj_ref, OFF_C, C_WIDTH, cx_halo)
    s["cx_halo"][...] = proj_ref[TOK - CONV_HALO:TOK, OFF_C:OFF_C + C_WIDTH]
    taps = [jnp.broadcast_to(cw[k:k + 1, :], (SEGS, C_WIDTH)) for k in range(CONV_WIDTH)]
    cbias = jnp.broadcast_to(_vec(vec, "half_conv_b"), (SEGS, C_WIDTH))
    xrow = {r: _seg_row(seg_in, r, all_slabs) for r in range(CONV_HALO - (CONV_WIDTH - 1), SEG_PITCH)}
    conv_rows = []
    for j in range(SEG_ROWS):
        acc = cbias + xrow[j + CONV_HALO] * taps[CONV_WIDTH - 1]
        for k in range(CONV_WIDTH - 1):
            acc = acc + xrow[j + CONV_HALO - (CONV_WIDTH - 1 - k)] * taps[k]
        conv_rows.append(acc)
    half_conv_all = jnp.concatenate(conv_rows, axis=0)
    yield
    for d in range(2):
        lo = d * C_HALF
        half_conv = half_conv_all[:, lo:lo + C_HALF]
        half_pre = jnp.dot(half_conv.astype(_BF16), p["gate_w"][d], preferred_element_type=_F32)
        t_r = jnp.tanh(half_pre[:, 0:C_HALF] + _vec(vec, "half_b_a")[:, lo:lo + C_HALF])
        t_i = jnp.tanh(half_pre[:, C_HALF:] + _vec(vec, "half_b_x")[:, lo:lo + C_HALF])
        half_nsp = half_neg_c_softplus[:, lo:lo + C_HALF]
        log_a = half_nsp * t_r + half_nsp
        a = jnp.exp(log_a)
        msq = jnp.tanh(log_a) * (-1.0 - a * a)
        mult = msq * jnp.minimum(lax.rsqrt(msq), F32_MAX)
        bterm = mult * (half_conv * t_i + half_conv)
        yield
        slabs = all_slabs[d * (C_HALF // LANES):(d + 1) * (C_HALF // LANES)]
        s["h_carry"][:, lo:lo + C_HALF] = _segment_scan(a, bterm, hc[:, lo:lo + C_HALF], seg_out, slabs)
        yield
    yc = _rms_scale(_seg_load_time_order(seg_out, all_slabs))
    yc = yc * _silu_of_half(proj_ref[:, OFF_C + C_WIDTH:OFF_C + 2 * C_WIDTH])
    y_ref[:, A_WIDTH + B_WIDTH:] = yc.astype(_BF16)
    yield


def _matmul_tiles(lhs_ref, w_tile_refs, emit):
    def tile(ti):
        cols = slice(ti * MXU_COLS, (ti + 1) * MXU_COLS)
        emit(cols, jnp.dot(lhs_ref[...], w_tile_refs[ti][...], preferred_element_type=_F32))
    return [functools.partial(tile, ti) for ti in range(len(w_tile_refs))]


_PARAM_NAMES = ("vec", "sgu_w", "sgu_bias", "pool_w", "gate_w")
N_IN_TILES = D_IN // MXU_COLS
N_OUT_TILES = D_MODEL // MXU_COLS
_STATE_NAMES = ("xb_ext", "s2_ext", "s4_ext", "s8_ext", "cx_halo", "h_carry", "seg_in", "seg_out")


def _layer_kernel(*refs, is_last, chunks_per_seq):
    x_ref, xlag_ref = refs[0], refs[1]
    n = 2 + len(_PARAM_NAMES)
    p = dict(zip(_PARAM_NAMES, refs[2:n]))
    w_in_tiles = refs[n:n + N_IN_TILES]
    w_out_tiles = refs[n + N_IN_TILES:n + N_IN_TILES + N_OUT_TILES]
    n += N_IN_TILES + N_OUT_TILES
    o_ref = refs[n]
    scratch = refs[n + 1:]
    proj_bufs = scratch[0:PROJ_SLOTS]
    y_bufs = scratch[PROJ_SLOTS:PROJ_SLOTS + STEP_CHUNKS]
    hn_buf = scratch[PROJ_SLOTS + STEP_CHUNKS]
    s = dict(zip(_STATE_NAMES, scratch[PROJ_SLOTS + STEP_CHUNKS + 1:]))
    k = pl.program_id(0)

    @pl.when(k == 0)
    def _init():
        for ref in scratch:
            ref[...] = jnp.zeros(ref.shape, ref.dtype)

    def chunk_slot(j, project, mix_it):
        rows = slice(j * TOK, (j + 1) * TOK)
        proj_ref = proj_bufs[j % PROJ_SLOTS]

        def emit_out(cols, t):
            o_ref[rows, cols] = xlag_ref[rows, cols] + t

        def emit_proj(cols, t):
            proj_ref[:, cols] = t

        mxu_work = _matmul_tiles(y_bufs[j], w_out_tiles, emit_out)
        if project:
            mxu_work += _matmul_tiles(hn_buf, w_in_tiles, emit_proj)
        mix = iter(())
        if mix_it:
            cm = lax.rem(STEP_CHUNKS * k + j + chunks_per_seq - 1, chunks_per_seq)
            mix = _mix_pieces(proj_bufs[(j - 1) % PROJ_SLOTS], y_bufs[(j - 1) % STEP_CHUNKS], cm, p, s)

        for t, thunk in enumerate(mxu_work):
            if t == 0 and project:
                hn_buf[...] = _rms_scale(x_ref[rows, :]).astype(_BF16)
            else:
                next(mix, None)
            thunk()
            if is_last and t == N_OUT_TILES - 1:
                o_ref[rows, :] = _rmsnorm(o_ref[rows, :], _vec(p["vec"], "final_g"))
        for _ in mix:
            pass

    last_step = pl.num_programs(0) - 1

    @pl.when(k < last_step)
    def _steady():
        for j in range(STEP_CHUNKS):
            chunk_slot(j, project=True, mix_it=True)

    @pl.when(k == last_step)
    def _drain():
        for j in range(STEP_CHUNKS):
            chunk_slot(j, project=False, mix_it=(j == 0))


def _block_diag(w):
    n, d = w.shape[-3], w.shape[-2]
    eye = jnp.eye(n, dtype=w.dtype)
    out = eye[:, None, :, None] * w[..., :, :, None, :]
    return out.reshape(w.shape[:-3] + (n * d, n * d))


def _prepare_params(norm_g, w_in, sgu_norm_g, sgu_w, sgu_b, pool_w, pool_b, pool_scale, conv_w, conv_b,
                    lru_wa, lru_ba, lru_wx, lru_bx, lru_lambda, branch_norm_g, w_out, final_g):
    depth = norm_g.shape[0]
    f32 = lambda a: a.astype(_F32)
    hph = C_HEADS // 2
    both = jnp.stack([f32(lru_wa), f32(lru_wx)], axis=1)
    both = both.reshape(depth, 2, 2, hph, C_HEAD_DIM, C_HEAD_DIM)
    gate_w = jnp.einsum("lpdgio,gh->ldgipho", both, jnp.eye(hph, dtype=_F32))
    gate_w = gate_w.reshape(depth, 2, C_HALF, 2 * C_HALF)
    half_gate_cols = jnp.concatenate([
        jnp.ones((2 * A_WIDTH,)), jnp.full((A_WIDTH,), 0.5), jnp.ones((B_WIDTH,)), jnp.full((B_WIDTH,), 0.5),
        jnp.ones((C_WIDTH,)), jnp.full((C_WIDTH,), 0.5)]).astype(_F32)
    sgu_bias = jnp.repeat(jnp.swapaxes(sgu_b, 1, 2), A_HEAD_DIM, axis=2).astype(_F32)

    def vec_row(*parts):
        row = jnp.concatenate([f32(a) for a in parts], axis=-1)
        return jnp.pad(row, ((0, 0), (0, D_MODEL - row.shape[-1])))

    vec = jnp.stack(
        [vec_row(jnp.broadcast_to(final_g, (depth, D_MODEL))),
         vec_row(sgu_norm_g, pool_b, pool_scale),
         vec_row(0.5 * conv_b, lru_lambda),
         vec_row(0.5 * lru_ba, 0.5 * lru_bx)]
        + [vec_row(0.5 * conv_w[:, k]) for k in range(CONV_WIDTH)], axis=1)
    small = dict(vec=vec, sgu_w=f32(sgu_w), sgu_bias=sgu_bias, pool_w=_block_diag(pool_w).astype(_BF16),
                 gate_w=gate_w.astype(_BF16))
    w_in_bf = (f32(w_in) * f32(norm_g)[:, :, None] * half_gate_cols).astype(_BF16)
    w_out_bf = (f32(w_out) * f32(branch_norm_g)[:, :, None]).astype(_BF16)
    return small, w_in_bf, w_out_bf


def _layer(x, small, w_in_bf, w_out_bf, layer, *, is_last, name):
    bsz, seq, _ = x.shape
    step_rows = STEP_CHUNKS * TOK
    assert seq % step_rows == 0 and TOK % SGU_BLOCK == 0 and STEP_CHUNKS % PROJ_SLOTS == 0
    steps_per_seq = seq // step_rows
    n_real = bsz * steps_per_seq
    small_ops = tuple(small[n] for n in _PARAM_NAMES)
    operands = (x, x) + small_ops + (w_in_bf,) * N_IN_TILES + (w_out_bf,) * N_OUT_TILES

    def x_map(k):
        kk = jnp.minimum(k, n_real - 1)
        return (kk // steps_per_seq, kk % steps_per_seq, 0)

    def lag_map(k):
        kl = jnp.maximum(k - LAG_STEPS, 0)
        return (kl // steps_per_seq, kl % steps_per_seq, 0)

    def of_layer(shape):
        return pl.BlockSpec((None,) + tuple(shape[1:]), lambda k: (layer,) + (0,) * (len(shape) - 1),
                            pipeline_mode=pl.Buffered(1))

    in_specs = [pl.BlockSpec((None, step_rows, D_MODEL), x_map),
                pl.BlockSpec((None, step_rows, D_MODEL), lag_map)]
    in_specs += [of_layer(op.shape) for op in small_ops]
    for w, tiles in ((w_in_bf, N_IN_TILES), (w_out_bf, N_OUT_TILES)):
        in_specs += [pl.BlockSpec((None, w.shape[1], MXU_COLS), functools.partial(lambda k, ti: (layer, 0, ti), ti=ti),
                                  pipeline_mode=pl.Buffered(1)) for ti in range(tiles)]
    scratch_shapes = (
        [pltpu.VMEM((TOK, D_IN), _F32) for _ in range(PROJ_SLOTS)]
        + [pltpu.VMEM((TOK, D_MODEL), _BF16) for _ in range(STEP_CHUNKS)]
        + [pltpu.VMEM((TOK, D_MODEL), _BF16),
           pltpu.VMEM((TOK + POOL_HALO, B_WIDTH), _F32),
           pltpu.VMEM((TOK + POOL_HALO, B_WIDTH), _F32),
           pltpu.VMEM((TOK + POOL_HALO, LANES), _F32),
           pltpu.VMEM((TOK + POOL_HALO, LANES), _F32),
           pltpu.VMEM((CONV_HALO, C_WIDTH), _F32),
           pltpu.VMEM((SUBLANES, C_WIDTH), _F32),
           pltpu.VMEM((C_WIDTH // LANES, SEGS * SEG_PITCH, LANES), _F32),
           pltpu.VMEM((C_WIDTH // LANES, SEGS * SEG_PITCH, LANES), _F32)])
    return pl.pallas_call(
        functools.partial(_layer_kernel, is_last=is_last, chunks_per_seq=seq // TOK),
        out_shape=jax.ShapeDtypeStruct(x.shape, x.dtype),
        grid=(n_real + LAG_STEPS,),
        in_specs=in_specs,
        out_specs=pl.BlockSpec((None, step_rows, D_MODEL), lag_map),
        scratch_shapes=scratch_shapes,
        compiler_params=pltpu.CompilerParams(
            dimension_semantics=("arbitrary",),
            vmem_limit_bytes=VMEM_LIMIT_BYTES,
        ),
        name=name,
    )(*operands)


def kernel(x, norm_g, w_in, sgu_norm_g, sgu_w, sgu_b, pool_w, pool_b, pool_scale, conv_w, conv_b,
           lru_wa, lru_ba, lru_wx, lru_bx, lru_lambda, branch_norm_g, w_out, final_g):
    small, w_in_bf, w_out_bf = _prepare_params(
        norm_g, w_in, sgu_norm_g, sgu_w, sgu_b, pool_w, pool_b, pool_scale, conv_w, conv_b,
        lru_wa, lru_ba, lru_wx, lru_bx, lru_lambda, branch_norm_g, w_out, final_g)
    for l in range(DEPTH):
        x = _layer(x, small, w_in_bf, w_out_bf, l, is_last=(l == DEPTH - 1), name=f"hybrid_layer{l}")
    return x
```

```python
import functools

import jax
import jax.numpy as jnp
from jax import lax
from jax.experimental import pallas as pl
from jax.experimental.pallas import tpu as pltpu

D_MODEL = 1024
DEPTH = 2
CHUNK = 64
EPS = 1e-6
A_WIDTH = 256
A_HEADS = 4
A_HEAD_DIM = 64
SGU_BLOCK = 128
B_WIDTH = 256
POOL_WINDOWS = (2, 4, 8, 16)
B_GROUP_DIM = 64
C_WIDTH = 512
C_HEADS = 8
C_HEAD_DIM = 64
CONV_WIDTH = 4
LRU_C = 8.0
D_IN = 3 * A_WIDTH + 2 * B_WIDTH + 2 * C_WIDTH
OFF_B = 3 * A_WIDTH
OFF_C = OFF_B + 2 * B_WIDTH

SUBLANES = 8
LANES = 128
MXU_COLS = 256
TOK = 256
STEP_CHUNKS = 4
PROJ_SLOTS = 2
LAG_STEPS = 1
POOL_HALO = 16
CONV_HALO = 8
C_HALF = C_WIDTH // 2
SEGS = SUBLANES
SEG_ROWS = TOK // SEGS
SEG_PITCH = SEG_ROWS + CONV_HALO
VMEM_LIMIT_BYTES = 56 * 1024 * 1024

_BF16 = jnp.bfloat16
_F32 = jnp.float32
F32_MAX = float(jnp.finfo(jnp.float32).max)
NEG_LOG2_E = -1.4426950408889634


def _rms_scale(x):
    ms = jnp.mean(x * x, axis=-1, keepdims=True)
    return x * lax.rsqrt(ms + EPS)


def _rmsnorm(x, g):
    return _rms_scale(x) * g


VEC_ROWS = 8
_VEC_SLOTS = {
    "final_g": (0, 0, D_MODEL),
    "sgu_g": (1, 0, A_WIDTH), "pool_b": (1, A_WIDTH, B_WIDTH), "pool_scale": (1, A_WIDTH + B_WIDTH, B_WIDTH),
    "half_conv_b": (2, 0, C_WIDTH), "lam": (2, C_WIDTH, C_WIDTH),
    "half_b_a": (3, 0, C_WIDTH), "half_b_x": (3, C_WIDTH, C_WIDTH),
}
VEC_CONV_W_ROW = 4


def _vec(vec_ref, name):
    r, c0, width = _VEC_SLOTS[name]
    return vec_ref[r:r + 1, c0:c0 + width]


def _silu_of_half(h):
    return h * jnp.tanh(h) + h


def _seg_store(seg_ref, x_ref, col0, width, halo):
    for slab in range(width // LANES):
        lanes = slice(col0 + slab * LANES, col0 + (slab + 1) * LANES)
        for g in range(SEGS):
            base = g * SEG_PITCH
            if g == 0:
                before = halo[:, slab * LANES:(slab + 1) * LANES]
            else:
                before = x_ref[g * SEG_ROWS - CONV_HALO:g * SEG_ROWS, lanes]
            seg_ref[slab, base:base + CONV_HALO, :] = before
            seg_ref[slab, base + CONV_HALO:base + SEG_PITCH, :] = x_ref[g * SEG_ROWS:(g + 1) * SEG_ROWS, lanes]


def _seg_row(seg_ref, r, slabs):
    return jnp.concatenate([seg_ref[slab, pl.ds(r, SEGS, stride=SEG_PITCH), :] for slab in slabs], axis=1)


def _segment_scan(a, b, h0, seg_ref, slabs):
    width = a.shape[-1]
    a3 = a.reshape(SEG_ROWS, SEGS, width)
    b3 = b.reshape(SEG_ROWS, SEGS, width)
    acum, hloc = [a3[0]], [b3[0]]
    for j in range(1, SEG_ROWS):
        acum.append(acum[-1] * a3[j])
        hloc.append(a3[j] * hloc[-1] + b3[j])
    end_a, end_h = acum[-1], hloc[-1]
    sub = lax.broadcasted_iota(jnp.int32, (SEGS, width), 0)
    carry = h0
    for _ in range(SEGS - 1):
        carry = jnp.where(sub == 0, h0, pltpu.roll(end_h + end_a * carry, 1, axis=0))
    for j in range(SEG_ROWS):
        hj = hloc[j] + acum[j] * carry
        for n, slab in enumerate(slabs):
            seg_ref[slab, pl.ds(j, SEGS, stride=SEG_PITCH), :] = hj[:, n * LANES:(n + 1) * LANES]
    last = end_h + end_a * carry
    return jnp.broadcast_to(last[SEGS - 1:SEGS, :], (SEGS, width))


def _seg_load_time_order(seg_ref, slabs):
    return jnp.concatenate(
        [jnp.concatenate([seg_ref[slab, g * SEG_PITCH:g * SEG_PITCH + SEG_ROWS, :] for g in range(SEGS)], axis=0)
         for slab in slabs], axis=1)


def _mix_pieces(proj_ref, y_ref, cm, p, s):
    first = cm == 0
    vec = p["vec"]

    for name in ("xb_ext", "s2_ext", "s4_ext", "s8_ext"):
        ref = s[name]
        ref[0:POOL_HALO, :] = jnp.where(first, 0.0, ref[0:POOL_HALO, :])
    cx_halo = jnp.where(first, 0.0, s["cx_halo"][...])
    hc = jnp.where(first, 0.0, s["h_carry"][...])

    v = _rmsnorm(proj_ref[:, A_WIDTH:2 * A_WIDTH], _vec(vec, "sgu_g")).astype(_BF16)
    ci = lax.broadcasted_iota(jnp.int32, (SGU_BLOCK, SGU_BLOCK), 0) // CHUNK
    cj = lax.broadcasted_iota(jnp.int32, (SGU_BLOCK, SGU_BLOCK), 1) // CHUNK
    mask = (cj <= ci).astype(_F32)
    wm = jnp.concatenate([(p["sgu_w"][h] * mask).astype(_BF16) for h in range(A_HEADS)], axis=1)
    head_of_lane = lax.broadcasted_iota(jnp.int32, (1, A_WIDTH), 1) // A_HEAD_DIM
    ys = []
    for n in range(TOK // SGU_BLOCK):
        vn = v[n * SGU_BLOCK:(n + 1) * SGU_BLOCK, :]
        rhs = jnp.concatenate([jnp.where(head_of_lane == h, vn, jnp.zeros_like(vn)) for h in range(A_HEADS)], axis=0)
        ys.append(jnp.dot(wm, rhs, preferred_element_type=_F32) + p["sgu_bias"][...])
    ya = proj_ref[:, 0:A_WIDTH] * jnp.concatenate(ys, axis=0)
    yield
    ya = _rms_scale(ya) * _silu_of_half(proj_ref[:, 2 * A_WIDTH:3 * A_WIDTH])
    y_ref[:, 0:A_WIDTH] = ya.astype(_BF16)
    yield

    xb_ext, s2_ext, s4_ext, s8_ext = s["xb_ext"], s["s2_ext"], s["s4_ext"], s["s8_ext"]
    bx = proj_ref[:, OFF_B:OFF_B + B_WIDTH]
    xb_ext[POOL_HALO:POOL_HALO + TOK, :] = bx
    s2 = bx + xb_ext[pl.ds(POOL_HALO - 1, TOK), :]
    s2_ext[POOL_HALO:POOL_HALO + TOK, :] = s2
    s4 = s2 + s2_ext[pl.ds(POOL_HALO - 2, TOK), :]
    s4_hi = s4[:, LANES:]
    s4_ext[POOL_HALO:POOL_HALO + TOK, :] = s4_hi
    s8 = s4_hi + s4_ext[pl.ds(POOL_HALO - 4, TOK), :]
    s8_ext[POOL_HALO:POOL_HALO + TOK, :] = s8
    s16 = s8 + s8_ext[pl.ds(POOL_HALO - 8, TOK), :]
    lane = lax.broadcasted_iota(jnp.int32, (1, LANES), 1)
    wsum = jnp.concatenate([jnp.where(lane < B_GROUP_DIM, s2[:, :LANES], s4[:, :LANES]),
                            jnp.where(lane < B_GROUP_DIM, s8, s16)], axis=1)
    lane_b = lax.broadcasted_iota(jnp.int32, (1, B_WIDTH), 1) // B_GROUP_DIM
    win = jnp.where(lane_b == 0, POOL_WINDOWS[0],
                    jnp.where(lane_b == 1, POOL_WINDOWS[1],
                              jnp.where(lane_b == 2, POOL_WINDOWS[2], POOL_WINDOWS[3])))
    inv_win = 1.0 / win.astype(_F32)
    t_head = cm * TOK + lax.broadcasted_iota(jnp.int32, (POOL_HALO, B_WIDTH), 0)
    cnt_head = jnp.minimum(t_head + 1, win).astype(_F32)
    mean = jnp.concatenate([wsum[0:POOL_HALO] / cnt_head, wsum[POOL_HALO:] * inv_win], axis=0)
    pooled = (mean - bx).astype(_BF16)
    for ref in (xb_ext, s2_ext, s4_ext, s8_ext):
        ref[0:POOL_HALO, :] = ref[TOK:TOK + POOL_HALO, :]
    yield
    yb = jnp.dot(pooled, p["pool_w"][...], preferred_element_type=_F32) + _vec(vec, "pool_b")
    yb = _rms_scale(yb * _vec(vec, "pool_scale"))
    yb = yb * _silu_of_half(proj_ref[:, OFF_B + B_WIDTH:OFF_B + 2 * B_WIDTH])
    y_ref[:, A_WIDTH:A_WIDTH + B_WIDTH] = yb.astype(_BF16)
    yield

    cw = vec[VEC_CONV_W_ROW:VEC_CONV_W_ROW + CONV_WIDTH, 0:C_WIDTH]
    half_c_softplus = (0.5 * LRU_C) * jax.nn.softplus(-_vec(vec, "lam"))
    seg_in, seg_out = s["seg_in"], s["seg_out"]
    all_slabs = tuple(range(C_WIDTH // LANES))
    _seg_store(seg_in, proj_ref, OFF_C, C_WIDTH, cx_halo)
    s["cx_halo"][...] = proj_ref[TOK - CONV_HALO:TOK, OFF_C:OFF_C + C_WIDTH]
    taps = [jnp.broadcast_to(cw[k:k + 1, :], (SEGS, C_WIDTH)) for k in range(CONV_WIDTH)]
    cbias = jnp.broadcast_to(_vec(vec, "half_conv_b"), (SEGS, C_WIDTH))
    xrow = {r: _seg_row(seg_in, r, all_slabs) for r in range(CONV_HALO - (CONV_WIDTH - 1), SEG_PITCH)}
    conv_rows = []
    for j in range(SEG_ROWS):
        acc = cbias + xrow[j + CONV_HALO] * taps[CONV_WIDTH - 1]
        for k in range(CONV_WIDTH - 1):
            acc = acc + xrow[j + CONV_HALO - (CONV_WIDTH - 1 - k)] * taps[k]
        conv_rows.append(acc)
    half_conv_all = jnp.concatenate(conv_rows, axis=0)
    yield
    for d in range(2):
        lo = d * C_HALF
        half_conv = half_conv_all[:, lo:lo + C_HALF]
        half_pre = jnp.dot(half_conv.astype(_BF16), p["gate_w"][d], preferred_element_type=_F32)
        t_r = jnp.tanh(half_pre[:, 0:C_HALF] + _vec(vec, "half_b_a")[:, lo:lo + C_HALF])
        t_i = jnp.tanh(half_pre[:, C_HALF:] + _vec(vec, "half_b_x")[:, lo:lo + C_HALF])
        half_csp = half_c_softplus[:, lo:lo + C_HALF]
        neg_log_a = half_csp * t_r + half_csp
        a = jnp.exp2(neg_log_a * NEG_LOG2_E)
        msq = jnp.tanh(neg_log_a) * (a * a + 1.0)
        mult = msq * jnp.minimum(lax.rsqrt(msq), F32_MAX)
        bterm = mult * (half_conv * t_i + half_conv)
        yield
        slabs = all_slabs[d * (C_HALF // LANES):(d + 1) * (C_HALF // LANES)]
        s["h_carry"][:, lo:lo + C_HALF] = _segment_scan(a, bterm, hc[:, lo:lo + C_HALF], seg_out, slabs)
        yield
    yc = _rms_scale(_seg_load_time_order(seg_out, all_slabs))
    yc = yc * _silu_of_half(proj_ref[:, OFF_C + C_WIDTH:OFF_C + 2 * C_WIDTH])
    y_ref[:, A_WIDTH + B_WIDTH:] = yc.astype(_BF16)
    yield


def _matmul_tiles(lhs_ref, w_tile_refs, emit):
    def tile(ti):
        cols = slice(ti * MXU_COLS, (ti + 1) * MXU_COLS)
        emit(cols, jnp.dot(lhs_ref[...], w_tile_refs[ti][...], preferred_element_type=_F32))
    return [functools.partial(tile, ti) for ti in range(len(w_tile_refs))]


_PARAM_NAMES = ("vec", "sgu_w", "sgu_bias", "pool_w", "gate_w")
N_IN_TILES = D_IN // MXU_COLS
N_OUT_TILES = D_MODEL // MXU_COLS
_STATE_NAMES = ("xb_ext", "s2_ext", "s4_ext", "s8_ext", "cx_halo", "h_carry", "seg_in", "seg_out")


def _layer_kernel(*refs, is_last, chunks_per_seq):
    x_ref, xlag_ref = refs[0], refs[1]
    n = 2 + len(_PARAM_NAMES)
    p = dict(zip(_PARAM_NAMES, refs[2:n]))
    w_in_tiles = refs[n:n + N_IN_TILES]
    w_out_tiles = refs[n + N_IN_TILES:n + N_IN_TILES + N_OUT_TILES]
    n += N_IN_TILES + N_OUT_TILES
    o_ref = refs[n]
    scratch = refs[n + 1:]
    proj_bufs = scratch[0:PROJ_SLOTS]
    y_bufs = scratch[PROJ_SLOTS:PROJ_SLOTS + STEP_CHUNKS]
    hn_buf = scratch[PROJ_SLOTS + STEP_CHUNKS]
    s = dict(zip(_STATE_NAMES, scratch[PROJ_SLOTS + STEP_CHUNKS + 1:]))
    k = pl.program_id(0)

    @pl.when(k == 0)
    def _init():
        for ref in scratch:
            ref[...] = jnp.zeros(ref.shape, ref.dtype)

    def chunk_slot(j, project, mix_it):
        rows = slice(j * TOK, (j + 1) * TOK)
        proj_ref = proj_bufs[j % PROJ_SLOTS]

        def emit_out(cols, t):
            o_ref[rows, cols] = xlag_ref[rows, cols] + t

        def emit_proj(cols, t):
            proj_ref[:, cols] = t

        mxu_work = _matmul_tiles(y_bufs[j], w_out_tiles, emit_out)
        if project:
            mxu_work += _matmul_tiles(hn_buf, w_in_tiles, emit_proj)
        mix = iter(())
        if mix_it:
            cm = lax.rem(STEP_CHUNKS * k + j + chunks_per_seq - 1, chunks_per_seq)
            mix = _mix_pieces(proj_bufs[(j - 1) % PROJ_SLOTS], y_bufs[(j - 1) % STEP_CHUNKS], cm, p, s)

        for t, thunk in enumerate(mxu_work):
            if t == 0 and project:
                hn_buf[...] = _rms_scale(x_ref[rows, :]).astype(_BF16)
            else:
                next(mix, None)
            thunk()
            if is_last and t == N_OUT_TILES - 1:
                o_ref[rows, :] = _rmsnorm(o_ref[rows, :], _vec(p["vec"], "final_g"))
        for _ in mix:
            pass

    last_step = pl.num_programs(0) - 1

    @pl.when(k < last_step)
    def _steady():
        for j in range(STEP_CHUNKS):
            chunk_slot(j, project=True, mix_it=True)

    @pl.when(k == last_step)
    def _drain():
        for j in range(STEP_CHUNKS):
            chunk_slot(j, project=False, mix_it=(j == 0))


def _block_diag(w):
    n, d = w.shape[-3], w.shape[-2]
    eye = jnp.eye(n, dtype=w.dtype)
    out = eye[:, None, :, None] * w[..., :, :, None, :]
    return out.reshape(w.shape[:-3] + (n * d, n * d))


def _prepare_params(norm_g, w_in, sgu_norm_g, sgu_w, sgu_b, pool_w, pool_b, pool_scale, conv_w, conv_b,
                    lru_wa, lru_ba, lru_wx, lru_bx, lru_lambda, branch_norm_g, w_out, final_g):
    depth = norm_g.shape[0]
    f32 = lambda a: a.astype(_F32)
    hph = C_HEADS // 2
    both = jnp.stack([f32(lru_wa), f32(lru_wx)], axis=1)
    both = both.reshape(depth, 2, 2, hph, C_HEAD_DIM, C_HEAD_DIM)
    gate_w = jnp.einsum("lpdgio,gh->ldgipho", both, jnp.eye(hph, dtype=_F32))
    gate_w = gate_w.reshape(depth, 2, C_HALF, 2 * C_HALF)
    half_gate_cols = jnp.concatenate([
        jnp.ones((2 * A_WIDTH,)), jnp.full((A_WIDTH,), 0.5), jnp.ones((B_WIDTH,)), jnp.full((B_WIDTH,), 0.5),
        jnp.ones((C_WIDTH,)), jnp.full((C_WIDTH,), 0.5)]).astype(_F32)
    sgu_bias = jnp.repeat(jnp.swapaxes(sgu_b, 1, 2), A_HEAD_DIM, axis=2).astype(_F32)

    def vec_row(*parts):
        row = jnp.concatenate([f32(a) for a in parts], axis=-1)
        return jnp.pad(row, ((0, 0), (0, D_MODEL - row.shape[-1])))

    vec = jnp.stack(
        [vec_row(jnp.broadcast_to(final_g, (depth, D_MODEL))),
         vec_row(sgu_norm_g, pool_b, pool_scale),
         vec_row(0.5 * conv_b, lru_lambda),
         vec_row(0.5 * lru_ba, 0.5 * lru_bx)]
        + [vec_row(0.5 * conv_w[:, k]) for k in range(CONV_WIDTH)], axis=1)
    small = dict(vec=vec, sgu_w=f32(sgu_w), sgu_bias=sgu_bias, pool_w=_block_diag(pool_w).astype(_BF16),
                 gate_w=gate_w.astype(_BF16))
    w_in_bf = (f32(w_in) * f32(norm_g)[:, :, None] * half_gate_cols).astype(_BF16)
    w_out_bf = (f32(w_out) * f32(branch_norm_g)[:, :, None]).astype(_BF16)
    return small, w_in_bf, w_out_bf


def _layer(x, small, w_in_bf, w_out_bf, layer, *, is_last, name):
    bsz, seq, _ = x.shape
    step_rows = STEP_CHUNKS * TOK
    assert seq % step_rows == 0 and TOK % SGU_BLOCK == 0 and STEP_CHUNKS % PROJ_SLOTS == 0
    steps_per_seq = seq // step_rows
    n_real = bsz * steps_per_seq
    small_ops = tuple(small[n] for n in _PARAM_NAMES)
    operands = (x, x) + small_ops + (w_in_bf,) * N_IN_TILES + (w_out_bf,) * N_OUT_TILES

    def x_map(k):
        kk = jnp.minimum(k, n_real - 1)
        return (kk // steps_per_seq, kk % steps_per_seq, 0)

    def lag_map(k):
        kl = jnp.maximum(k - LAG_STEPS, 0)
        return (kl // steps_per_seq, kl % steps_per_seq, 0)

    def of_layer(shape):
        return pl.BlockSpec((None,) + tuple(shape[1:]), lambda k: (layer,) + (0,) * (len(shape) - 1),
                            pipeline_mode=pl.Buffered(1))

    in_specs = [pl.BlockSpec((None, step_rows, D_MODEL), x_map),
                pl.BlockSpec((None, step_rows, D_MODEL), lag_map)]
    in_specs += [of_layer(op.shape) for op in small_ops]
    for w, tiles in ((w_in_bf, N_IN_TILES), (w_out_bf, N_OUT_TILES)):
        in_specs += [pl.BlockSpec((None, w.shape[1], MXU_COLS), functools.partial(lambda k, ti: (layer, 0, ti), ti=ti),
                                  pipeline_mode=pl.Buffered(1)) for ti in range(tiles)]
    scratch_shapes = (
        [pltpu.VMEM((TOK, D_IN), _F32) for _ in range(PROJ_SLOTS)]
        + [pltpu.VMEM((TOK, D_MODEL), _BF16) for _ in range(STEP_CHUNKS)]
        + [pltpu.VMEM((TOK, D_MODEL), _BF16),
           pltpu.VMEM((TOK + POOL_HALO, B_WIDTH), _F32),
           pltpu.VMEM((TOK + POOL_HALO, B_WIDTH), _F32),
           pltpu.VMEM((TOK + POOL_HALO, LANES), _F32),
           pltpu.VMEM((TOK + POOL_HALO, LANES), _F32),
           pltpu.VMEM((CONV_HALO, C_WIDTH), _F32),
           pltpu.VMEM((SUBLANES, C_WIDTH), _F32),
           pltpu.VMEM((C_WIDTH // LANES, SEGS * SEG_PITCH, LANES), _F32),
           pltpu.VMEM((C_WIDTH // LANES, SEGS * SEG_PITCH, LANES), _F32)])
    return pl.pallas_call(
        functools.partial(_layer_kernel, is_last=is_last, chunks_per_seq=seq // TOK),
        out_shape=jax.ShapeDtypeStruct(x.shape, x.dtype),
        grid=(n_real + LAG_STEPS,),
        in_specs=in_specs,
        out_specs=pl.BlockSpec((None, step_rows, D_MODEL), lag_map),
        scratch_shapes=scratch_shapes,
        compiler_params=pltpu.CompilerParams(
            dimension_semantics=("arbitrary",),
            vmem_limit_bytes=VMEM_LIMIT_BYTES,
        ),
        name=name,
    )(*operands)


def kernel(x, norm_g, w_in, sgu_norm_g, sgu_w, sgu_b, pool_w, pool_b, pool_scale, conv_w, conv_b,
           lru_wa, lru_ba, lru_wx, lru_bx, lru_lambda, branch_norm_g, w_out, final_g):
    small, w_in_bf, w_out_bf = _prepare_params(
        norm_g, w_in, sgu_norm_g, sgu_w, sgu_b, pool_w, pool_b, pool_scale, conv_w, conv_b,
        lru_wa, lru_ba, lru_wx, lru_bx, lru_lambda, branch_norm_g, w_out, final_g)
    for l in range(DEPTH):
        x = _layer(x, small, w_in_bf, w_out_bf, l, is_last=(l == DEPTH - 1), name=f"hybrid_layer{l}")
    return x
```

```python
import functools

import jax
import jax.numpy as jnp
from jax import lax
from jax.experimental import pallas as pl
from jax.experimental.pallas import tpu as pltpu

D_MODEL = 1024
DEPTH = 2
CHUNK = 64
EPS = 1e-6
A_WIDTH = 256
A_HEADS = 4
A_HEAD_DIM = 64
SGU_BLOCK = 128
B_WIDTH = 256
POOL_WINDOWS = (2, 4, 8, 16)
B_GROUP_DIM = 64
C_WIDTH = 512
C_HEADS = 8
C_HEAD_DIM = 64
CONV_WIDTH = 4
LRU_C = 8.0
D_IN = 3 * A_WIDTH + 2 * B_WIDTH + 2 * C_WIDTH
OFF_B = 3 * A_WIDTH
OFF_C = OFF_B + 2 * B_WIDTH

SUBLANES = 8
LANES = 128
MXU_COLS = 256
TOK = 256
STEP_CHUNKS = 4
PROJ_SLOTS = 2
LAG_STEPS = 1
POOL_HALO = 16
CONV_HALO = 8
C_HALF = C_WIDTH // 2
SEGS = SUBLANES
SEG_ROWS = TOK // SEGS
SEG_PITCH = SEG_ROWS + CONV_HALO
VMEM_LIMIT_BYTES = 56 * 1024 * 1024

_BF16 = jnp.bfloat16
_F32 = jnp.float32
F32_MAX = float(jnp.finfo(jnp.float32).max)
NEG_LOG2_E = -1.4426950408889634


def _rms_scale(x):
    ms = jnp.mean(x * x, axis=-1, keepdims=True)
    return x * lax.rsqrt(ms + EPS)


def _rmsnorm(x, g):
    return _rms_scale(x) * g


VEC_ROWS = 8
_VEC_SLOTS = {
    "final_g": (0, 0, D_MODEL),
    "sgu_g": (1, 0, A_WIDTH), "pool_bias": (1, A_WIDTH, B_WIDTH),
    "half_conv_b": (2, 0, C_WIDTH), "lam": (2, C_WIDTH, C_WIDTH),
    "half_b_a": (3, 0, C_WIDTH), "half_b_x": (3, C_WIDTH, C_WIDTH),
}
VEC_CONV_W_ROW = 4


def _vec(vec_ref, name):
    r, c0, width = _VEC_SLOTS[name]
    return vec_ref[r:r + 1, c0:c0 + width]


def _silu_of_half(h):
    return h * jnp.tanh(h) + h


def _seg_store(seg_ref, x_ref, col0, width, halo):
    for slab in range(width // LANES):
        lanes = slice(col0 + slab * LANES, col0 + (slab + 1) * LANES)
        for g in range(SEGS):
            base = g * SEG_PITCH
            if g == 0:
                before = halo[:, slab * LANES:(slab + 1) * LANES]
            else:
                before = x_ref[g * SEG_ROWS - CONV_HALO:g * SEG_ROWS, lanes]
            seg_ref[slab, base:base + CONV_HALO, :] = before
            seg_ref[slab, base + CONV_HALO:base + SEG_PITCH, :] = x_ref[g * SEG_ROWS:(g + 1) * SEG_ROWS, lanes]


def _seg_row(seg_ref, r, slabs):
    return jnp.concatenate([seg_ref[slab, pl.ds(r, SEGS, stride=SEG_PITCH), :] for slab in slabs], axis=1)


def _segment_scan(a, b, h0, seg_ref, slabs):
    width = a.shape[-1]
    a3 = a.reshape(SEG_ROWS, SEGS, width)
    b3 = b.reshape(SEG_ROWS, SEGS, width)
    acum, hloc = [a3[0]], [b3[0]]
    for j in range(1, SEG_ROWS):
        acum.append(acum[-1] * a3[j])
        hloc.append(a3[j] * hloc[-1] + b3[j])
    end_a, end_h = acum[-1], hloc[-1]
    sub = lax.broadcasted_iota(jnp.int32, (SEGS, width), 0)
    carry = h0
    for _ in range(SEGS - 1):
        carry = jnp.where(sub == 0, h0, pltpu.roll(end_h + end_a * carry, 1, axis=0))
    for j in range(SEG_ROWS):
        hj = hloc[j] + acum[j] * carry
        for n, slab in enumerate(slabs):
            seg_ref[slab, pl.ds(j, SEGS, stride=SEG_PITCH), :] = hj[:, n * LANES:(n + 1) * LANES]
    last = end_h + end_a * carry
    return jnp.broadcast_to(last[SEGS - 1:SEGS, :], (SEGS, width))


def _seg_load_time_order(seg_ref, slabs):
    return jnp.concatenate(
        [jnp.concatenate([seg_ref[slab, g * SEG_PITCH:g * SEG_PITCH + SEG_ROWS, :] for g in range(SEGS)], axis=0)
         for slab in slabs], axis=1)


def _mix_pieces(proj_ref, y_ref, cm, p, s):
    first = cm == 0
    vec = p["vec"]

    for name in ("xb_ext", "s2_ext", "s4_ext", "s8_ext"):
        ref = s[name]
        ref[0:POOL_HALO, :] = jnp.where(first, 0.0, ref[0:POOL_HALO, :])
    cx_halo = jnp.where(first, 0.0, s["cx_halo"][...])
    hc = jnp.where(first, 0.0, s["h_carry"][...])

    v = _rmsnorm(proj_ref[:, A_WIDTH:2 * A_WIDTH], _vec(vec, "sgu_g")).astype(_BF16)
    ci = lax.broadcasted_iota(jnp.int32, (SGU_BLOCK, SGU_BLOCK), 0) // CHUNK
    cj = lax.broadcasted_iota(jnp.int32, (SGU_BLOCK, SGU_BLOCK), 1) // CHUNK
    mask = (cj <= ci).astype(_F32)
    wm = jnp.concatenate([(p["sgu_w"][h] * mask).astype(_BF16) for h in range(A_HEADS)], axis=1)
    head_of_lane = lax.broadcasted_iota(jnp.int32, (1, A_WIDTH), 1) // A_HEAD_DIM
    ys = []
    for n in range(TOK // SGU_BLOCK):
        vn = v[n * SGU_BLOCK:(n + 1) * SGU_BLOCK, :]
        rhs = jnp.concatenate([jnp.where(head_of_lane == h, vn, jnp.zeros_like(vn)) for h in range(A_HEADS)], axis=0)
        ys.append(jnp.dot(wm, rhs, preferred_element_type=_F32) + p["sgu_bias"][...])
    ya = proj_ref[:, 0:A_WIDTH] * jnp.concatenate(ys, axis=0)
    yield
    ya = _rms_scale(ya) * _silu_of_half(proj_ref[:, 2 * A_WIDTH:3 * A_WIDTH])
    y_ref[:, 0:A_WIDTH] = ya.astype(_BF16)
    yield

    xb_ext, s2_ext, s4_ext, s8_ext = s["xb_ext"], s["s2_ext"], s["s4_ext"], s["s8_ext"]
    bx = proj_ref[:, OFF_B:OFF_B + B_WIDTH]
    xb_ext[POOL_HALO:POOL_HALO + TOK, :] = bx
    s2 = bx + xb_ext[pl.ds(POOL_HALO - 1, TOK), :]
    s2_ext[POOL_HALO:POOL_HALO + TOK, :] = s2
    s4 = s2 + s2_ext[pl.ds(POOL_HALO - 2, TOK), :]
    s4_hi = s4[:, LANES:]
    s4_ext[POOL_HALO:POOL_HALO + TOK, :] = s4_hi
    s8 = s4_hi + s4_ext[pl.ds(POOL_HALO - 4, TOK), :]
    s8_ext[POOL_HALO:POOL_HALO + TOK, :] = s8
    s16 = s8 + s8_ext[pl.ds(POOL_HALO - 8, TOK), :]
    lane = lax.broadcasted_iota(jnp.int32, (1, LANES), 1)
    wsum = jnp.concatenate([jnp.where(lane < B_GROUP_DIM, s2[:, :LANES], s4[:, :LANES]),
                            jnp.where(lane < B_GROUP_DIM, s8, s16)], axis=1)
    lane_b = lax.broadcasted_iota(jnp.int32, (1, B_WIDTH), 1) // B_GROUP_DIM
    win = jnp.where(lane_b == 0, POOL_WINDOWS[0],
                    jnp.where(lane_b == 1, POOL_WINDOWS[1],
                              jnp.where(lane_b == 2, POOL_WINDOWS[2], POOL_WINDOWS[3])))
    inv_win = 1.0 / win.astype(_F32)
    t_head = cm * TOK + lax.broadcasted_iota(jnp.int32, (POOL_HALO, B_WIDTH), 0)
    cnt_head = jnp.minimum(t_head + 1, win).astype(_F32)
    mean = jnp.concatenate([wsum[0:POOL_HALO] / cnt_head, wsum[POOL_HALO:] * inv_win], axis=0)
    yb = (mean - bx) + _vec(vec, "pool_bias")
    for ref in (xb_ext, s2_ext, s4_ext, s8_ext):
        ref[0:POOL_HALO, :] = ref[TOK:TOK + POOL_HALO, :]
    yield
    yb = _rms_scale(yb) * _silu_of_half(proj_ref[:, OFF_B + B_WIDTH:OFF_B + 2 * B_WIDTH])
    y_ref[:, A_WIDTH:A_WIDTH + B_WIDTH] = yb.astype(_BF16)
    yield

    cw = vec[VEC_CONV_W_ROW:VEC_CONV_W_ROW + CONV_WIDTH, 0:C_WIDTH]
    half_c_softplus = (0.5 * LRU_C) * jax.nn.softplus(-_vec(vec, "lam"))
    seg_in, seg_out = s["seg_in"], s["seg_out"]
    all_slabs = tuple(range(C_WIDTH // LANES))
    _seg_store(seg_in, proj_ref, OFF_C, C_WIDTH, cx_halo)
    s["cx_halo"][...] = proj_ref[TOK - CONV_HALO:TOK, OFF_C:OFF_C + C_WIDTH]
    taps = [jnp.broadcast_to(cw[k:k + 1, :], (SEGS, C_WIDTH)) for k in range(CONV_WIDTH)]
    cbias = jnp.broadcast_to(_vec(vec, "half_conv_b"), (SEGS, C_WIDTH))
    xrow = {r: _seg_row(seg_in, r, all_slabs) for r in range(CONV_HALO - (CONV_WIDTH - 1), SEG_PITCH)}
    conv_rows = []
    for j in range(SEG_ROWS):
        acc = cbias + xrow[j + CONV_HALO] * taps[CONV_WIDTH - 1]
        for k in range(CONV_WIDTH - 1):
            acc = acc + xrow[j + CONV_HALO - (CONV_WIDTH - 1 - k)] * taps[k]
        conv_rows.append(acc)
    half_conv_all = jnp.concatenate(conv_rows, axis=0)
    yield
    for d in range(2):
        lo = d * C_HALF
        half_conv = half_conv_all[:, lo:lo + C_HALF]
        half_pre = jnp.dot(half_conv.astype(_BF16), p["gate_w"][d], preferred_element_type=_F32)
        t_r = jnp.tanh(half_pre[:, 0:C_HALF] + _vec(vec, "half_b_a")[:, lo:lo + C_HALF])
        t_i = jnp.tanh(half_pre[:, C_HALF:] + _vec(vec, "half_b_x")[:, lo:lo + C_HALF])
        half_csp = half_c_softplus[:, lo:lo + C_HALF]
        neg_log_a = half_csp * t_r + half_csp
        a = jnp.exp2(neg_log_a * NEG_LOG2_E)
        msq = jnp.tanh(neg_log_a) * (a * a + 1.0)
        mult = msq * jnp.minimum(lax.rsqrt(msq), F32_MAX)
        bterm = mult * (half_conv * t_i + half_conv)
        yield
        slabs = all_slabs[d * (C_HALF // LANES):(d + 1) * (C_HALF // LANES)]
        s["h_carry"][:, lo:lo + C_HALF] = _segment_scan(a, bterm, hc[:, lo:lo + C_HALF], seg_out, slabs)
        yield
    yc = _rms_scale(_seg_load_time_order(seg_out, all_slabs))
    yc = yc * _silu_of_half(proj_ref[:, OFF_C + C_WIDTH:OFF_C + 2 * C_WIDTH])
    y_ref[:, A_WIDTH + B_WIDTH:] = yc.astype(_BF16)
    yield


def _matmul_tiles(lhs_ref, w_tile_refs, emit):
    def tile(ti):
        cols = slice(ti * MXU_COLS, (ti + 1) * MXU_COLS)
        emit(cols, jnp.dot(lhs_ref[...], w_tile_refs[ti][...], preferred_element_type=_F32))
    return [functools.partial(tile, ti) for ti in range(len(w_tile_refs))]


_PARAM_NAMES = ("vec", "sgu_w", "sgu_bias", "gate_w")
N_IN_TILES = D_IN // MXU_COLS
N_OUT_TILES = D_MODEL // MXU_COLS
_STATE_NAMES = ("xb_ext", "s2_ext", "s4_ext", "s8_ext", "cx_halo", "h_carry", "seg_in", "seg_out")


def _layer_kernel(*refs, is_last, chunks_per_seq):
    x_ref, xlag_ref = refs[0], refs[1]
    n = 2 + len(_PARAM_NAMES)
    p = dict(zip(_PARAM_NAMES, refs[2:n]))
    w_in_tiles = refs[n:n + N_IN_TILES]
    w_out_tiles = refs[n + N_IN_TILES:n + N_IN_TILES + N_OUT_TILES]
    n += N_IN_TILES + N_OUT_TILES
    o_ref = refs[n]
    scratch = refs[n + 1:]
    proj_bufs = scratch[0:PROJ_SLOTS]
    y_bufs = scratch[PROJ_SLOTS:PROJ_SLOTS + STEP_CHUNKS]
    hn_buf = scratch[PROJ_SLOTS + STEP_CHUNKS]
    s = dict(zip(_STATE_NAMES, scratch[PROJ_SLOTS + STEP_CHUNKS + 1:]))
    k = pl.program_id(0)

    @pl.when(k == 0)
    def _init():
        for ref in scratch:
            ref[...] = jnp.zeros(ref.shape, ref.dtype)

    def chunk_slot(j, project, mix_it):
        rows = slice(j * TOK, (j + 1) * TOK)
        proj_ref = proj_bufs[j % PROJ_SLOTS]

        def emit_out(cols, t):
            o_ref[rows, cols] = xlag_ref[rows, cols] + t

        def emit_proj(cols, t):
            proj_ref[:, cols] = t

        mxu_work = _matmul_tiles(y_bufs[j], w_out_tiles, emit_out)
        if project:
            mxu_work += _matmul_tiles(hn_buf, w_in_tiles, emit_proj)
        mix = iter(())
        if mix_it:
            cm = lax.rem(STEP_CHUNKS * k + j + chunks_per_seq - 1, chunks_per_seq)
            mix = _mix_pieces(proj_bufs[(j - 1) % PROJ_SLOTS], y_bufs[(j - 1) % STEP_CHUNKS], cm, p, s)

        for t, thunk in enumerate(mxu_work):
            if t == 0 and project:
                hn_buf[...] = _rms_scale(x_ref[rows, :]).astype(_BF16)
            else:
                next(mix, None)
            thunk()
            if is_last and t == N_OUT_TILES - 1:
                o_ref[rows, :] = _rmsnorm(o_ref[rows, :], _vec(p["vec"], "final_g"))
        for _ in mix:
            pass

    last_step = pl.num_programs(0) - 1

    @pl.when(k < last_step)
    def _steady():
        for j in range(STEP_CHUNKS):
            chunk_slot(j, project=True, mix_it=True)

    @pl.when(k == last_step)
    def _drain():
        for j in range(STEP_CHUNKS):
            chunk_slot(j, project=False, mix_it=(j == 0))


def _block_diag(w):
    n, d = w.shape[-3], w.shape[-2]
    eye = jnp.eye(n, dtype=w.dtype)
    out = eye[:, None, :, None] * w[..., :, :, None, :]
    return out.reshape(w.shape[:-3] + (n * d, n * d))


def _prepare_params(norm_g, w_in, sgu_norm_g, sgu_w, sgu_b, pool_w, pool_b, pool_scale, conv_w, conv_b,
                    lru_wa, lru_ba, lru_wx, lru_bx, lru_lambda, branch_norm_g, w_out, final_g):
    depth = norm_g.shape[0]
    f32 = lambda a: a.astype(_F32)
    hph = C_HEADS // 2
    both = jnp.stack([f32(lru_wa), f32(lru_wx)], axis=1)
    both = both.reshape(depth, 2, 2, hph, C_HEAD_DIM, C_HEAD_DIM)
    gate_w = jnp.einsum("lpdgio,gh->ldgipho", both, jnp.eye(hph, dtype=_F32))
    gate_w = gate_w.reshape(depth, 2, C_HALF, 2 * C_HALF)
    half_gate_cols = jnp.concatenate([
        jnp.ones((2 * A_WIDTH,)), jnp.full((A_WIDTH,), 0.5), jnp.ones((B_WIDTH,)), jnp.full((B_WIDTH,), 0.5),
        jnp.ones((C_WIDTH,)), jnp.full((C_WIDTH,), 0.5)]).astype(_F32)
    sgu_bias = jnp.repeat(jnp.swapaxes(sgu_b, 1, 2), A_HEAD_DIM, axis=2).astype(_F32)

    def vec_row(*parts):
        row = jnp.concatenate([f32(a) for a in parts], axis=-1)
        return jnp.pad(row, ((0, 0), (0, D_MODEL - row.shape[-1])))

    vec = jnp.stack(
        [vec_row(jnp.broadcast_to(final_g, (depth, D_MODEL))),
         vec_row(sgu_norm_g, pool_b * pool_scale),
         vec_row(0.5 * conv_b, lru_lambda),
         vec_row(0.5 * lru_ba, 0.5 * lru_bx)]
        + [vec_row(0.5 * conv_w[:, k]) for k in range(CONV_WIDTH)], axis=1)
    small = dict(vec=vec, sgu_w=f32(sgu_w), sgu_bias=sgu_bias, gate_w=gate_w.astype(_BF16))
    w_bx = f32(w_in)[:, :, OFF_B:OFF_B + B_WIDTH]
    w_bx = jnp.einsum("lkc,lcd->lkd", w_bx, _block_diag(f32(pool_w)), precision=lax.Precision.HIGHEST)
    w_in = f32(w_in).at[:, :, OFF_B:OFF_B + B_WIDTH].set(w_bx * f32(pool_scale)[:, None, :])
    w_in_bf = (f32(w_in) * f32(norm_g)[:, :, None] * half_gate_cols).astype(_BF16)
    w_out_bf = (f32(w_out) * f32(branch_norm_g)[:, :, None]).astype(_BF16)
    return small, w_in_bf, w_out_bf


def _layer(x, small, w_in_bf, w_out_bf, layer, *, is_last, name):
    bsz, seq, _ = x.shape
    step_rows = STEP_CHUNKS * TOK
    assert seq % step_rows == 0 and TOK % SGU_BLOCK == 0 and STEP_CHUNKS % PROJ_SLOTS == 0
    steps_per_seq = seq // step_rows
    n_real = bsz * steps_per_seq
    small_ops = tuple(small[n] for n in _PARAM_NAMES)
    operands = (x, x) + small_ops + (w_in_bf,) * N_IN_TILES + (w_out_bf,) * N_OUT_TILES

    def x_map(k):
        kk = jnp.minimum(k, n_real - 1)
        return (kk // steps_per_seq, kk % steps_per_seq, 0)

    def lag_map(k):
        kl = jnp.maximum(k - LAG_STEPS, 0)
        return (kl // steps_per_seq, kl % steps_per_seq, 0)

    def of_layer(shape):
        return pl.BlockSpec((None,) + tuple(shape[1:]), lambda k: (layer,) + (0,) * (len(shape) - 1),
                            pipeline_mode=pl.Buffered(1))

    in_specs = [pl.BlockSpec((None, step_rows, D_MODEL), x_map),
                pl.BlockSpec((None, step_rows, D_MODEL), lag_map)]
    in_specs += [of_layer(op.shape) for op in small_ops]
    for w, tiles in ((w_in_bf, N_IN_TILES), (w_out_bf, N_OUT_TILES)):
        in_specs += [pl.BlockSpec((None, w.shape[1], MXU_COLS), functools.partial(lambda k, ti: (layer, 0, ti), ti=ti),
                                  pipeline_mode=pl.Buffered(1)) for ti in range(tiles)]
    scratch_shapes = (
        [pltpu.VMEM((TOK, D_IN), _F32) for _ in range(PROJ_SLOTS)]
        + [pltpu.VMEM((TOK, D_MODEL), _BF16) for _ in range(STEP_CHUNKS)]
        + [pltpu.VMEM((TOK, D_MODEL), _BF16),
           pltpu.VMEM((TOK + POOL_HALO, B_WIDTH), _F32),
           pltpu.VMEM((TOK + POOL_HALO, B_WIDTH), _F32),
           pltpu.VMEM((TOK + POOL_HALO, LANES), _F32),
           pltpu.VMEM((TOK + POOL_HALO, LANES), _F32),
           pltpu.VMEM((CONV_HALO, C_WIDTH), _F32),
           pltpu.VMEM((SUBLANES, C_WIDTH), _F32),
           pltpu.VMEM((C_WIDTH // LANES, SEGS * SEG_PITCH, LANES), _F32),
           pltpu.VMEM((C_WIDTH // LANES, SEGS * SEG_PITCH, LANES), _F32)])
    return pl.pallas_call(
        functools.partial(_layer_kernel, is_last=is_last, chunks_per_seq=seq // TOK),
        out_shape=jax.ShapeDtypeStruct(x.shape, x.dtype),
        grid=(n_real + LAG_STEPS,),
        in_specs=in_specs,
        out_specs=pl.BlockSpec((None, step_rows, D_MODEL), lag_map),
        scratch_shapes=scratch_shapes,
        compiler_params=pltpu.CompilerParams(
            dimension_semantics=("arbitrary",),
            vmem_limit_bytes=VMEM_LIMIT_BYTES,
        ),
        name=name,
    )(*operands)


def kernel(x, norm_g, w_in, sgu_norm_g, sgu_w, sgu_b, pool_w, pool_b, pool_scale, conv_w, conv_b,
           lru_wa, lru_ba, lru_wx, lru_bx, lru_lambda, branch_norm_g, w_out, final_g):
    small, w_in_bf, w_out_bf = _prepare_params(
        norm_g, w_in, sgu_norm_g, sgu_w, sgu_b, pool_w, pool_b, pool_scale, conv_w, conv_b,
        lru_wa, lru_ba, lru_wx, lru_bx, lru_lambda, branch_norm_g, w_out, final_g)
    for l in range(DEPTH):
        x = _layer(x, small, w_in_bf, w_out_bf, l, is_last=(l == DEPTH - 1), name=f"hybrid_layer{l}")
    return x
```

```python
import functools

import jax
import jax.numpy as jnp
from jax import lax
from jax.experimental import pallas as pl
from jax.experimental.pallas import tpu as pltpu

D_MODEL = 1024
DEPTH = 2
CHUNK = 64
EPS = 1e-6
A_WIDTH = 256
A_HEADS = 4
A_HEAD_DIM = 64
SGU_BLOCK = 128
B_WIDTH = 256
POOL_WINDOWS = (2, 4, 8, 16)
B_GROUP_DIM = 64
C_WIDTH = 512
C_HEADS = 8
C_HEAD_DIM = 64
CONV_WIDTH = 4
LRU_C = 8.0
D_IN = 3 * A_WIDTH + 2 * B_WIDTH + 2 * C_WIDTH
OFF_B = 3 * A_WIDTH
OFF_C = OFF_B + 2 * B_WIDTH

SUBLANES = 8
LANES = 128
MXU_COLS = 256
TOK = 256
STEP_CHUNKS = 4
PROJ_SLOTS = 2
LAG_STEPS = 1
POOL_HALO = 16
CONV_HALO = 8
C_HALF = C_WIDTH // 2
SEGS = SUBLANES
SEG_ROWS = TOK // SEGS
SEG_PITCH = SEG_ROWS + CONV_HALO
VMEM_LIMIT_BYTES = 56 * 1024 * 1024

_BF16 = jnp.bfloat16
_F32 = jnp.float32
F32_MAX = float(jnp.finfo(jnp.float32).max)
NEG_LOG2_E = -1.4426950408889634


def _rms_scale(x):
    ms = jnp.mean(x * x, axis=-1, keepdims=True)
    return x * lax.rsqrt(ms + EPS)


def _rmsnorm(x, g):
    return _rms_scale(x) * g


VEC_ROWS = 8
_VEC_SLOTS = {
    "final_g": (0, 0, D_MODEL),
    "sgu_g": (1, 0, A_WIDTH), "pool_b": (1, A_WIDTH, B_WIDTH), "pool_scale": (1, A_WIDTH + B_WIDTH, B_WIDTH),
    "half_conv_b": (2, 0, C_WIDTH), "lam": (2, C_WIDTH, C_WIDTH),
    "half_b_a": (3, 0, C_WIDTH), "half_b_x": (3, C_WIDTH, C_WIDTH),
}
VEC_CONV_W_ROW = 4


def _vec(vec_ref, name):
    r, c0, width = _VEC_SLOTS[name]
    return vec_ref[r:r + 1, c0:c0 + width]


def _silu_of_half(h):
    return h * jnp.tanh(h) + h


def _seg_store(seg_ref, x_ref, col0, width, halo):
    for slab in range(width // LANES):
        lanes = slice(col0 + slab * LANES, col0 + (slab + 1) * LANES)
        for g in range(SEGS):
            base = g * SEG_PITCH
            if g == 0:
                before = halo[:, slab * LANES:(slab + 1) * LANES]
            else:
                before = x_ref[g * SEG_ROWS - CONV_HALO:g * SEG_ROWS, lanes]
            seg_ref[slab, base:base + CONV_HALO, :] = before
            seg_ref[slab, base + CONV_HALO:base + SEG_PITCH, :] = x_ref[g * SEG_ROWS:(g + 1) * SEG_ROWS, lanes]


def _seg_row(seg_ref, r, slabs):
    return jnp.concatenate([seg_ref[slab, pl.ds(r, SEGS, stride=SEG_PITCH), :] for slab in slabs], axis=1)


def _segment_scan(a, b, h0, seg_ref, slabs):
    width = a.shape[-1]
    a3 = a.reshape(SEG_ROWS, SEGS, width)
    b3 = b.reshape(SEG_ROWS, SEGS, width)
    acum, hloc = [a3[0]], [b3[0]]
    for j in range(1, SEG_ROWS):
        acum.append(acum[-1] * a3[j])
        hloc.append(a3[j] * hloc[-1] + b3[j])
    end_a, end_h = acum[-1], hloc[-1]
    sub = lax.broadcasted_iota(jnp.int32, (SEGS, width), 0)
    carry = h0
    for _ in range(SEGS - 1):
        carry = jnp.where(sub == 0, h0, pltpu.roll(end_h + end_a * carry, 1, axis=0))
    for j in range(SEG_ROWS):
        hj = hloc[j] + acum[j] * carry
        for n, slab in enumerate(slabs):
            seg_ref[slab, pl.ds(j, SEGS, stride=SEG_PITCH), :] = hj[:, n * LANES:(n + 1) * LANES]
    last = end_h + end_a * carry
    return jnp.broadcast_to(last[SEGS - 1:SEGS, :], (SEGS, width))


def _seg_load_time_order(seg_ref, slabs):
    return jnp.concatenate(
        [jnp.concatenate([seg_ref[slab, g * SEG_PITCH:g * SEG_PITCH + SEG_ROWS, :] for g in range(SEGS)], axis=0)
         for slab in slabs], axis=1)


def _mix_pieces(proj_ref, y_ref, cm, p, s):
    first = cm == 0
    vec = p["vec"]

    for name in ("xb_ext", "s2_ext", "s4_ext", "s8_ext"):
        ref = s[name]
        ref[0:POOL_HALO, :] = jnp.where(first, 0.0, ref[0:POOL_HALO, :])
    cx_halo = jnp.where(first, 0.0, s["cx_halo"][...])
    hc = jnp.where(first, 0.0, s["h_carry"][...])

    v = _rmsnorm(proj_ref[:, A_WIDTH:2 * A_WIDTH], _vec(vec, "sgu_g")).astype(_BF16)
    ci = lax.broadcasted_iota(jnp.int32, (SGU_BLOCK, SGU_BLOCK), 0) // CHUNK
    cj = lax.broadcasted_iota(jnp.int32, (SGU_BLOCK, SGU_BLOCK), 1) // CHUNK
    mask = (cj <= ci).astype(_F32)
    wm = jnp.concatenate([(p["sgu_w"][h] * mask).astype(_BF16) for h in range(A_HEADS)], axis=1)
    head_of_lane = lax.broadcasted_iota(jnp.int32, (1, A_WIDTH), 1) // A_HEAD_DIM
    ys = []
    for n in range(TOK // SGU_BLOCK):
        vn = v[n * SGU_BLOCK:(n + 1) * SGU_BLOCK, :]
        rhs = jnp.concatenate([jnp.where(head_of_lane == h, vn, jnp.zeros_like(vn)) for h in range(A_HEADS)], axis=0)
        ys.append(jnp.dot(wm, rhs, preferred_element_type=_F32) + p["sgu_bias"][...])
    ya = proj_ref[:, 0:A_WIDTH] * jnp.concatenate(ys, axis=0)
    yield
    ya = _rms_scale(ya) * _silu_of_half(proj_ref[:, 2 * A_WIDTH:3 * A_WIDTH])
    y_ref[:, 0:A_WIDTH] = ya.astype(_BF16)
    yield

    xb_ext, s2_ext, s4_ext, s8_ext = s["xb_ext"], s["s2_ext"], s["s4_ext"], s["s8_ext"]
    bx = proj_ref[:, OFF_B:OFF_B + B_WIDTH]
    xb_ext[POOL_HALO:POOL_HALO + TOK, :] = bx
    s2 = bx + xb_ext[pl.ds(POOL_HALO - 1, TOK), :]
    s2_ext[POOL_HALO:POOL_HALO + TOK, :] = s2
    s4 = s2 + s2_ext[pl.ds(POOL_HALO - 2, TOK), :]
    s4_hi = s4[:, LANES:]
    s4_ext[POOL_HALO:POOL_HALO + TOK, :] = s4_hi
    s8 = s4_hi + s4_ext[pl.ds(POOL_HALO - 4, TOK), :]
    s8_ext[POOL_HALO:POOL_HALO + TOK, :] = s8
    s16 = s8 + s8_ext[pl.ds(POOL_HALO - 8, TOK), :]
    lane = lax.broadcasted_iota(jnp.int32, (1, LANES), 1)
    wsum = jnp.concatenate([jnp.where(lane < B_GROUP_DIM, s2[:, :LANES], s4[:, :LANES]),
                            jnp.where(lane < B_GROUP_DIM, s8, s16)], axis=1)
    lane_b = lax.broadcasted_iota(jnp.int32, (1, B_WIDTH), 1) // B_GROUP_DIM
    win = jnp.where(lane_b == 0, POOL_WINDOWS[0],
                    jnp.where(lane_b == 1, POOL_WINDOWS[1],
                              jnp.where(lane_b == 2, POOL_WINDOWS[2], POOL_WINDOWS[3])))
    inv_win = 1.0 / win.astype(_F32)
    t_head = cm * TOK + lax.broadcasted_iota(jnp.int32, (POOL_HALO, B_WIDTH), 0)
    cnt_head = jnp.minimum(t_head + 1, win).astype(_F32)
    mean = jnp.concatenate([wsum[0:POOL_HALO] / cnt_head, wsum[POOL_HALO:] * inv_win], axis=0)
    pooled = (mean - bx).astype(_BF16)
    for ref in (xb_ext, s2_ext, s4_ext, s8_ext):
        ref[0:POOL_HALO, :] = ref[TOK:TOK + POOL_HALO, :]
    yield
    yb = jnp.dot(pooled, p["pool_w"][...], preferred_element_type=_F32) + _vec(vec, "pool_b")
    yb = _rms_scale(yb * _vec(vec, "pool_scale"))
    yb = yb * _silu_of_half(proj_ref[:, OFF_B + B_WIDTH:OFF_B + 2 * B_WIDTH])
    y_ref[:, A_WIDTH:A_WIDTH + B_WIDTH] = yb.astype(_BF16)
    yield

    cw = vec[VEC_CONV_W_ROW:VEC_CONV_W_ROW + CONV_WIDTH, 0:C_WIDTH]
    half_c_softplus = (0.5 * LRU_C) * jax.nn.softplus(-_vec(vec, "lam"))
    seg_in, seg_out = s["seg_in"], s["seg_out"]
    all_slabs = tuple(range(C_WIDTH // LANES))
    _seg_store(seg_in, proj_ref, OFF_C, C_WIDTH, cx_halo)
    s["cx_halo"][...] = proj_ref[TOK - CONV_HALO:TOK, OFF_C:OFF_C + C_WIDTH]
    taps = [jnp.broadcast_to(cw[k:k + 1, :], (SEGS, C_WIDTH)) for k in range(CONV_WIDTH)]
    cbias = jnp.broadcast_to(_vec(vec, "half_conv_b"), (SEGS, C_WIDTH))
    xrow = {r: _seg_row(seg_in, r, all_slabs) for r in range(CONV_HALO - (CONV_WIDTH - 1), SEG_PITCH)}
    conv_rows = []
    for j in range(SEG_ROWS):
        acc = cbias + xrow[j + CONV_HALO] * taps[CONV_WIDTH - 1]
        for k in range(CONV_WIDTH - 1):
            acc = acc + xrow[j + CONV_HALO - (CONV_WIDTH - 1 - k)] * taps[k]
        conv_rows.append(acc)
    half_conv_all = jnp.concatenate(conv_rows, axis=0)
    yield
    for d in range(2):
        lo = d * C_HALF
        half_conv = half_conv_all[:, lo:lo + C_HALF]
        half_pre = jnp.dot(half_conv.astype(_BF16), p["gate_w"][d], preferred_element_type=_F32)
        t_r = jnp.tanh(half_pre[:, 0:C_HALF] + _vec(vec, "half_b_a")[:, lo:lo + C_HALF])
        t_i = jnp.tanh(half_pre[:, C_HALF:] + _vec(vec, "half_b_x")[:, lo:lo + C_HALF])
        half_csp = half_c_softplus[:, lo:lo + C_HALF]
        neg_log_a = half_csp * t_r + half_csp
        a = jnp.exp2(neg_log_a * NEG_LOG2_E)
        msq = jnp.tanh(neg_log_a) * (a * a + 1.0)
        mult = msq * jnp.minimum(lax.rsqrt(msq), F32_MAX)
        bterm = mult * (half_conv * t_i + half_conv)
        yield
        slabs = all_slabs[d * (C_HALF // LANES):(d + 1) * (C_HALF // LANES)]
        s["h_carry"][:, lo:lo + C_HALF] = _segment_scan(a, bterm, hc[:, lo:lo + C_HALF], seg_out, slabs)
        yield
    yc = _rms_scale(_seg_load_time_order(seg_out, all_slabs))
    yc = yc * _silu_of_half(proj_ref[:, OFF_C + C_WIDTH:OFF_C + 2 * C_WIDTH])
    y_ref[:, A_WIDTH + B_WIDTH:] = yc.astype(_BF16)
    yield


def _matmul_tiles(lhs_ref, w_tile_refs, emit):
    def tile(ti):
        cols = slice(ti * MXU_COLS, (ti + 1) * MXU_COLS)
        emit(cols, jnp.dot(lhs_ref[...], w_tile_refs[ti][...], preferred_element_type=_F32))
    return [functools.partial(tile, ti) for ti in range(len(w_tile_refs))]


_PARAM_NAMES = ("vec", "sgu_w", "sgu_bias", "pool_w", "gate_w")
N_IN_TILES = D_IN // MXU_COLS
N_OUT_TILES = D_MODEL // MXU_COLS
_STATE_NAMES = ("xb_ext", "s2_ext", "s4_ext", "s8_ext", "cx_halo", "h_carry", "seg_in", "seg_out")


def _layer_kernel(*refs, is_last, chunks_per_seq):
    x_ref, xlag_ref = refs[0], refs[1]
    n = 2 + len(_PARAM_NAMES)
    p = dict(zip(_PARAM_NAMES, refs[2:n]))
    w_in_tiles = refs[n:n + N_IN_TILES]
    w_out_tiles = refs[n + N_IN_TILES:n + N_IN_TILES + N_OUT_TILES]
    n += N_IN_TILES + N_OUT_TILES
    o_ref = refs[n]
    scratch = refs[n + 1:]
    proj_bufs = scratch[0:PROJ_SLOTS]
    y_bufs = scratch[PROJ_SLOTS:PROJ_SLOTS + STEP_CHUNKS]
    hn_buf = scratch[PROJ_SLOTS + STEP_CHUNKS]
    s = dict(zip(_STATE_NAMES, scratch[PROJ_SLOTS + STEP_CHUNKS + 1:]))
    k = pl.program_id(0)

    @pl.when(k == 0)
    def _init():
        for name in ("xb_ext", "s2_ext", "s4_ext", "s8_ext", "cx_halo", "h_carry"):
            s[name][...] = jnp.zeros(s[name].shape, s[name].dtype)

    def chunk_slot(j, project, mix_it, output):
        rows = slice(j * TOK, (j + 1) * TOK)
        proj_ref = proj_bufs[j % PROJ_SLOTS]

        def emit_out(cols, t):
            o_ref[rows, cols] = xlag_ref[rows, cols] + t

        def emit_proj(cols, t):
            proj_ref[:, cols] = t

        mxu_work = []
        if output:
            mxu_work += _matmul_tiles(y_bufs[j], w_out_tiles, emit_out)
        n_out = len(mxu_work)
        if project:
            mxu_work += _matmul_tiles(hn_buf, w_in_tiles, emit_proj)
        mix = iter(())
        if mix_it:
            cm = lax.rem(STEP_CHUNKS * k + j + chunks_per_seq - 1, chunks_per_seq)
            mix = _mix_pieces(proj_bufs[(j - 1) % PROJ_SLOTS], y_bufs[(j - 1) % STEP_CHUNKS], cm, p, s)

        for t, thunk in enumerate(mxu_work):
            if t == 0 and project:
                hn_buf[...] = _rms_scale(x_ref[rows, :]).astype(_BF16)
            else:
                next(mix, None)
            thunk()
            if is_last and t == n_out - 1:
                o_ref[rows, :] = _rmsnorm(o_ref[rows, :], _vec(p["vec"], "final_g"))
        for _ in mix:
            pass

    last_step = pl.num_programs(0) - 1

    @pl.when(k == 0)
    def _fill():
        for j in range(STEP_CHUNKS):
            chunk_slot(j, project=True, mix_it=(j > 0), output=False)

    @pl.when((k > 0) & (k < last_step))
    def _steady():
        for j in range(STEP_CHUNKS):
            chunk_slot(j, project=True, mix_it=True, output=True)

    @pl.when(k == last_step)
    def _drain():
        for j in range(STEP_CHUNKS):
            chunk_slot(j, project=False, mix_it=(j == 0), output=True)


def _block_diag(w):
    n, d = w.shape[-3], w.shape[-2]
    eye = jnp.eye(n, dtype=w.dtype)
    out = eye[:, None, :, None] * w[..., :, :, None, :]
    return out.reshape(w.shape[:-3] + (n * d, n * d))


def _prepare_params(norm_g, w_in, sgu_norm_g, sgu_w, sgu_b, pool_w, pool_b, pool_scale, conv_w, conv_b,
                    lru_wa, lru_ba, lru_wx, lru_bx, lru_lambda, branch_norm_g, w_out, final_g):
    depth = norm_g.shape[0]
    f32 = lambda a: a.astype(_F32)
    hph = C_HEADS // 2
    both = jnp.stack([f32(lru_wa), f32(lru_wx)], axis=1)
    both = both.reshape(depth, 2, 2, hph, C_HEAD_DIM, C_HEAD_DIM)
    gate_w = jnp.einsum("lpdgio,gh->ldgipho", both, jnp.eye(hph, dtype=_F32))
    gate_w = gate_w.reshape(depth, 2, C_HALF, 2 * C_HALF)
    half_gate_cols = jnp.concatenate([
        jnp.ones((2 * A_WIDTH,)), jnp.full((A_WIDTH,), 0.5), jnp.ones((B_WIDTH,)), jnp.full((B_WIDTH,), 0.5),
        jnp.ones((C_WIDTH,)), jnp.full((C_WIDTH,), 0.5)]).astype(_F32)
    sgu_bias = jnp.repeat(jnp.swapaxes(sgu_b, 1, 2), A_HEAD_DIM, axis=2).astype(_F32)

    def vec_row(*parts):
        row = jnp.concatenate([f32(a) for a in parts], axis=-1)
        return jnp.pad(row, ((0, 0), (0, D_MODEL - row.shape[-1])))

    vec = jnp.stack(
        [vec_row(jnp.broadcast_to(final_g, (depth, D_MODEL))),
         vec_row(sgu_norm_g, pool_b, pool_scale),
         vec_row(0.5 * conv_b, lru_lambda),
         vec_row(0.5 * lru_ba, 0.5 * lru_bx)]
        + [vec_row(0.5 * conv_w[:, k]) for k in range(CONV_WIDTH)], axis=1)
    small = dict(vec=vec, sgu_w=f32(sgu_w), sgu_bias=sgu_bias, pool_w=_block_diag(pool_w).astype(_BF16),
                 gate_w=gate_w.astype(_BF16))
    w_in_bf = (f32(w_in) * f32(norm_g)[:, :, None] * half_gate_cols).astype(_BF16)
    w_out_bf = (f32(w_out) * f32(branch_norm_g)[:, :, None]).astype(_BF16)
    return small, w_in_bf, w_out_bf


def _layer(x, small, w_in_bf, w_out_bf, layer, *, is_last, name):
    bsz, seq, _ = x.shape
    step_rows = STEP_CHUNKS * TOK
    assert seq % step_rows == 0 and TOK % SGU_BLOCK == 0 and STEP_CHUNKS % PROJ_SLOTS == 0
    steps_per_seq = seq // step_rows
    n_real = bsz * steps_per_seq
    small_ops = tuple(small[n] for n in _PARAM_NAMES)
    operands = (x, x) + small_ops + (w_in_bf,) * N_IN_TILES + (w_out_bf,) * N_OUT_TILES

    def x_map(k):
        kk = jnp.minimum(k, n_real - 1)
        return (kk // steps_per_seq, kk % steps_per_seq, 0)

    def lag_map(k):
        kl = jnp.maximum(k - LAG_STEPS, 0)
        return (kl // steps_per_seq, kl % steps_per_seq, 0)

    def of_layer(shape):
        return pl.BlockSpec((None,) + tuple(shape[1:]), lambda k: (layer,) + (0,) * (len(shape) - 1),
                            pipeline_mode=pl.Buffered(1))

    in_specs = [pl.BlockSpec((None, step_rows, D_MODEL), x_map),
                pl.BlockSpec((None, step_rows, D_MODEL), lag_map)]
    in_specs += [of_layer(op.shape) for op in small_ops]
    for w, tiles in ((w_in_bf, N_IN_TILES), (w_out_bf, N_OUT_TILES)):
        in_specs += [pl.BlockSpec((None, w.shape[1], MXU_COLS), functools.partial(lambda k, ti: (layer, 0, ti), ti=ti),
                                  pipeline_mode=pl.Buffered(1)) for ti in range(tiles)]
    scratch_shapes = (
        [pltpu.VMEM((TOK, D_IN), _F32) for _ in range(PROJ_SLOTS)]
        + [pltpu.VMEM((TOK, D_MODEL), _BF16) for _ in range(STEP_CHUNKS)]
        + [pltpu.VMEM((TOK, D_MODEL), _BF16),
           pltpu.VMEM((TOK + POOL_HALO, B_WIDTH), _F32),
           pltpu.VMEM((TOK + POOL_HALO, B_WIDTH), _F32),
           pltpu.VMEM((TOK + POOL_HALO, LANES), _F32),
           pltpu.VMEM((TOK + POOL_HALO, LANES), _F32),
           pltpu.VMEM((CONV_HALO, C_WIDTH), _F32),
           pltpu.VMEM((SUBLANES, C_WIDTH), _F32),
           pltpu.VMEM((C_WIDTH // LANES, SEGS * SEG_PITCH, LANES), _F32),
           pltpu.VMEM((C_WIDTH // LANES, SEGS * SEG_PITCH, LANES), _F32)])
    return pl.pallas_call(
        functools.partial(_layer_kernel, is_last=is_last, chunks_per_seq=seq // TOK),
        out_shape=jax.ShapeDtypeStruct(x.shape, x.dtype),
        grid=(n_real + LAG_STEPS,),
        in_specs=in_specs,
        out_specs=pl.BlockSpec((None, step_rows, D_MODEL), lag_map),
        scratch_shapes=scratch_shapes,
        compiler_params=pltpu.CompilerParams(
            dimension_semantics=("arbitrary",),
            vmem_limit_bytes=VMEM_LIMIT_BYTES,
        ),
        name=name,
    )(*operands)


def kernel(x, norm_g, w_in, sgu_norm_g, sgu_w, sgu_b, pool_w, pool_b, pool_scale, conv_w, conv_b,
           lru_wa, lru_ba, lru_wx, lru_bx, lru_lambda, branch_norm_g, w_out, final_g):
    small, w_in_bf, w_out_bf = _prepare_params(
        norm_g, w_in, sgu_norm_g, sgu_w, sgu_b, pool_w, pool_b, pool_scale, conv_w, conv_b,
        lru_wa, lru_ba, lru_wx, lru_bx, lru_lambda, branch_norm_g, w_out, final_g)
    for l in range(DEPTH):
        x = _layer(x, small, w_in_bf, w_out_bf, l, is_last=(l == DEPTH - 1), name=f"hybrid_layer{l}")
    return x
```

```python
import functools

import jax
import jax.numpy as jnp
from jax import lax
from jax.experimental import pallas as pl
from jax.experimental.pallas import tpu as pltpu

D_MODEL = 1024
DEPTH = 2
CHUNK = 64
EPS = 1e-6
A_WIDTH = 256
A_HEADS = 4
A_HEAD_DIM = 64
SGU_BLOCK = 128
B_WIDTH = 256
POOL_WINDOWS = (2, 4, 8, 16)
B_GROUP_DIM = 64
C_WIDTH = 512
C_HEADS = 8
C_HEAD_DIM = 64
CONV_WIDTH = 4
LRU_C = 8.0
D_IN = 3 * A_WIDTH + 2 * B_WIDTH + 2 * C_WIDTH
OFF_B = 3 * A_WIDTH
OFF_C = OFF_B + 2 * B_WIDTH

SUBLANES = 8
LANES = 128
MXU_COLS = 256
TOK = 256
STEP_CHUNKS = 4
PROJ_SLOTS = 2
LAG_STEPS = 1
POOL_HALO = 16
CONV_HALO = 8
C_HALF = C_WIDTH // 2
SEGS = SUBLANES
SEG_ROWS = TOK // SEGS
SEG_PITCH = SEG_ROWS + CONV_HALO
VMEM_LIMIT_BYTES = 56 * 1024 * 1024

_BF16 = jnp.bfloat16
_F32 = jnp.float32
F32_MAX = float(jnp.finfo(jnp.float32).max)
NEG_LOG2_E = -1.4426950408889634


def _rms_scale(x):
    ms = jnp.mean(x * x, axis=-1, keepdims=True)
    return x * lax.rsqrt(ms + EPS)


def _rmsnorm(x, g):
    return _rms_scale(x) * g


VEC_ROWS = 8
_VEC_SLOTS = {
    "final_g": (0, 0, D_MODEL),
    "sgu_g": (1, 0, A_WIDTH), "pool_b": (1, A_WIDTH, B_WIDTH), "pool_scale": (1, A_WIDTH + B_WIDTH, B_WIDTH),
    "half_conv_b": (2, 0, C_WIDTH), "lam": (2, C_WIDTH, C_WIDTH),
    "half_b_a": (3, 0, C_WIDTH), "half_b_x": (3, C_WIDTH, C_WIDTH),
}
VEC_CONV_W_ROW = 4


def _vec(vec_ref, name):
    r, c0, width = _VEC_SLOTS[name]
    return vec_ref[r:r + 1, c0:c0 + width]


def _gated_bf16(u, h):
    hb = h.astype(_BF16)
    return u.astype(_BF16) * (hb * jnp.tanh(hb) + hb)


def _silu_of_half(h):
    return h * jnp.tanh(h) + h


def _seg_store(seg_ref, x_ref, col0, width, halo):
    for slab in range(width // LANES):
        lanes = slice(col0 + slab * LANES, col0 + (slab + 1) * LANES)
        for g in range(SEGS):
            base = g * SEG_PITCH
            if g == 0:
                before = halo[:, slab * LANES:(slab + 1) * LANES]
            else:
                before = x_ref[g * SEG_ROWS - CONV_HALO:g * SEG_ROWS, lanes]
            seg_ref[slab, base:base + CONV_HALO, :] = before
            seg_ref[slab, base + CONV_HALO:base + SEG_PITCH, :] = x_ref[g * SEG_ROWS:(g + 1) * SEG_ROWS, lanes]


def _seg_row(seg_ref, r, slabs):
    return jnp.concatenate([seg_ref[slab, pl.ds(r, SEGS, stride=SEG_PITCH), :] for slab in slabs], axis=1)


def _segment_scan(a, b, h0, seg_ref, slabs):
    width = a.shape[-1]
    a3 = a.reshape(SEG_ROWS, SEGS, width)
    b3 = b.reshape(SEG_ROWS, SEGS, width)
    acum, hloc = [a3[0]], [b3[0]]
    for j in range(1, SEG_ROWS):
        acum.append(acum[-1] * a3[j])
        hloc.append(a3[j] * hloc[-1] + b3[j])
    end_a, end_h = acum[-1], hloc[-1]
    sub = lax.broadcasted_iota(jnp.int32, (SEGS, width), 0)
    carry = h0
    for _ in range(SEGS - 1):
        carry = jnp.where(sub == 0, h0, pltpu.roll(end_h + end_a * carry, 1, axis=0))
    for j in range(SEG_ROWS):
        hj = hloc[j] + acum[j] * carry
        for n, slab in enumerate(slabs):
            seg_ref[slab, pl.ds(j, SEGS, stride=SEG_PITCH), :] = hj[:, n * LANES:(n + 1) * LANES]
    last = end_h + end_a * carry
    return jnp.broadcast_to(last[SEGS - 1:SEGS, :], (SEGS, width))


def _seg_load_time_order(seg_ref, slabs):
    return jnp.concatenate(
        [jnp.concatenate([seg_ref[slab, g * SEG_PITCH:g * SEG_PITCH + SEG_ROWS, :] for g in range(SEGS)], axis=0)
         for slab in slabs], axis=1)


def _mix_pieces(proj_ref, y_ref, cm, p, s):
    first = cm == 0
    vec = p["vec"]

    for name in ("xb_ext", "s2_ext", "s4_ext", "s8_ext"):
        ref = s[name]
        ref[0:POOL_HALO, :] = jnp.where(first, 0.0, ref[0:POOL_HALO, :])
    cx_halo = jnp.where(first, 0.0, s["cx_halo"][...])
    hc = jnp.where(first, 0.0, s["h_carry"][...])

    v = _rmsnorm(proj_ref[:, A_WIDTH:2 * A_WIDTH], _vec(vec, "sgu_g")).astype(_BF16)
    ci = lax.broadcasted_iota(jnp.int32, (SGU_BLOCK, SGU_BLOCK), 0) // CHUNK
    cj = lax.broadcasted_iota(jnp.int32, (SGU_BLOCK, SGU_BLOCK), 1) // CHUNK
    mask = (cj <= ci).astype(_F32)
    wm = jnp.concatenate([(p["sgu_w"][h] * mask).astype(_BF16) for h in range(A_HEADS)], axis=1)
    head_of_lane = lax.broadcasted_iota(jnp.int32, (1, A_WIDTH), 1) // A_HEAD_DIM
    ys = []
    for n in range(TOK // SGU_BLOCK):
        vn = v[n * SGU_BLOCK:(n + 1) * SGU_BLOCK, :]
        rhs = jnp.concatenate([jnp.where(head_of_lane == h, vn, jnp.zeros_like(vn)) for h in range(A_HEADS)], axis=0)
        ys.append(jnp.dot(wm, rhs, preferred_element_type=_F32) + p["sgu_bias"][...])
    ya = proj_ref[:, 0:A_WIDTH] * jnp.concatenate(ys, axis=0)
    yield
    y_ref[:, 0:A_WIDTH] = _gated_bf16(_rms_scale(ya), proj_ref[:, 2 * A_WIDTH:3 * A_WIDTH])
    yield

    xb_ext, s2_ext, s4_ext, s8_ext = s["xb_ext"], s["s2_ext"], s["s4_ext"], s["s8_ext"]
    bx = proj_ref[:, OFF_B:OFF_B + B_WIDTH]
    xb_ext[POOL_HALO:POOL_HALO + TOK, :] = bx
    s2 = bx + xb_ext[pl.ds(POOL_HALO - 1, TOK), :]
    s2_ext[POOL_HALO:POOL_HALO + TOK, :] = s2
    s4 = s2 + s2_ext[pl.ds(POOL_HALO - 2, TOK), :]
    s4_hi = s4[:, LANES:]
    s4_ext[POOL_HALO:POOL_HALO + TOK, :] = s4_hi
    s8 = s4_hi + s4_ext[pl.ds(POOL_HALO - 4, TOK), :]
    s8_ext[POOL_HALO:POOL_HALO + TOK, :] = s8
    s16 = s8 + s8_ext[pl.ds(POOL_HALO - 8, TOK), :]
    lane = lax.broadcasted_iota(jnp.int32, (1, LANES), 1)
    wsum = jnp.concatenate([jnp.where(lane < B_GROUP_DIM, s2[:, :LANES], s4[:, :LANES]),
                            jnp.where(lane < B_GROUP_DIM, s8, s16)], axis=1)
    lane_b = lax.broadcasted_iota(jnp.int32, (1, B_WIDTH), 1) // B_GROUP_DIM
    win = jnp.where(lane_b == 0, POOL_WINDOWS[0],
                    jnp.where(lane_b == 1, POOL_WINDOWS[1],
                              jnp.where(lane_b == 2, POOL_WINDOWS[2], POOL_WINDOWS[3])))
    inv_win = 1.0 / win.astype(_F32)
    t_head = cm * TOK + lax.broadcasted_iota(jnp.int32, (POOL_HALO, B_WIDTH), 0)
    cnt_head = jnp.minimum(t_head + 1, win).astype(_F32)
    mean = jnp.concatenate([wsum[0:POOL_HALO] / cnt_head, wsum[POOL_HALO:] * inv_win], axis=0)
    pooled = (mean - bx).astype(_BF16)
    for ref in (xb_ext, s2_ext, s4_ext, s8_ext):
        ref[0:POOL_HALO, :] = ref[TOK:TOK + POOL_HALO, :]
    yield
    yb = jnp.dot(pooled, p["pool_w"][...], preferred_element_type=_F32) + _vec(vec, "pool_b")
    yb = _rms_scale(yb * _vec(vec, "pool_scale"))
    y_ref[:, A_WIDTH:A_WIDTH + B_WIDTH] = _gated_bf16(yb, proj_ref[:, OFF_B + B_WIDTH:OFF_B + 2 * B_WIDTH])
    yield

    cw = vec[VEC_CONV_W_ROW:VEC_CONV_W_ROW + CONV_WIDTH, 0:C_WIDTH]
    half_c_softplus = (0.5 * LRU_C) * jax.nn.softplus(-_vec(vec, "lam"))
    seg_in, seg_out = s["seg_in"], s["seg_out"]
    all_slabs = tuple(range(C_WIDTH // LANES))
    _seg_store(seg_in, proj_ref, OFF_C, C_WIDTH, cx_halo)
    s["cx_halo"][...] = proj_ref[TOK - CONV_HALO:TOK, OFF_C:OFF_C + C_WIDTH]
    taps = [jnp.broadcast_to(cw[k:k + 1, :], (SEGS, C_WIDTH)) for k in range(CONV_WIDTH)]
    cbias = jnp.broadcast_to(_vec(vec, "half_conv_b"), (SEGS, C_WIDTH))
    xrow = {r: _seg_row(seg_in, r, all_slabs) for r in range(CONV_HALO - (CONV_WIDTH - 1), SEG_PITCH)}
    conv_rows = []
    for j in range(SEG_ROWS):
        acc = cbias + xrow[j + CONV_HALO] * taps[CONV_WIDTH - 1]
        for k in range(CONV_WIDTH - 1):
            acc = acc + xrow[j + CONV_HALO - (CONV_WIDTH - 1 - k)] * taps[k]
        conv_rows.append(acc)
    half_conv_all = jnp.concatenate(conv_rows, axis=0)
    yield
    for d in range(2):
        lo = d * C_HALF
        half_conv = half_conv_all[:, lo:lo + C_HALF]
        half_pre = jnp.dot(half_conv.astype(_BF16), p["gate_w"][d], preferred_element_type=_F32)
        t_r = jnp.tanh(half_pre[:, 0:C_HALF] + _vec(vec, "half_b_a")[:, lo:lo + C_HALF])
        t_i = jnp.tanh(half_pre[:, C_HALF:] + _vec(vec, "half_b_x")[:, lo:lo + C_HALF])
        half_csp = half_c_softplus[:, lo:lo + C_HALF]
        neg_log_a = half_csp * t_r + half_csp
        a = jnp.exp2(neg_log_a * NEG_LOG2_E)
        msq = jnp.tanh(neg_log_a) * (a * a + 1.0)
        mult = msq * jnp.minimum(lax.rsqrt(msq), F32_MAX)
        bterm = mult * (half_conv * t_i + half_conv)
        yield
        slabs = all_slabs[d * (C_HALF // LANES):(d + 1) * (C_HALF // LANES)]
        s["h_carry"][:, lo:lo + C_HALF] = _segment_scan(a, bterm, hc[:, lo:lo + C_HALF], seg_out, slabs)
        yield
    yc = _rms_scale(_seg_load_time_order(seg_out, all_slabs))
    y_ref[:, A_WIDTH + B_WIDTH:] = _gated_bf16(yc, proj_ref[:, OFF_C + C_WIDTH:OFF_C + 2 * C_WIDTH])
    yield


def _matmul_tiles(lhs_ref, w_tile_refs, emit):
    def tile(c0, ref):
        cols = slice(c0, c0 + ref.shape[-1])
        emit(cols, jnp.dot(lhs_ref[...], ref[...], preferred_element_type=_F32))
    out, c0 = [], 0
    for ref in w_tile_refs:
        out.append((ref.shape[-1], functools.partial(tile, c0, ref)))
        c0 += ref.shape[-1]
    return out


_PARAM_NAMES = ("vec", "sgu_w", "sgu_bias", "pool_w", "gate_w")
TILE_COLS = 2 * MXU_COLS
IN_TILE_WIDTHS = (TILE_COLS,) * (D_IN // TILE_COLS) + ((D_IN % TILE_COLS,) if D_IN % TILE_COLS else ())
OUT_TILE_WIDTHS = (TILE_COLS,) * (D_MODEL // TILE_COLS)
N_IN_TILES = len(IN_TILE_WIDTHS)
N_OUT_TILES = len(OUT_TILE_WIDTHS)
_STATE_NAMES = ("xb_ext", "s2_ext", "s4_ext", "s8_ext", "cx_halo", "h_carry", "seg_in", "seg_out")


def _layer_kernel(*refs, is_last, chunks_per_seq):
    x_ref, xlag_ref = refs[0], refs[1]
    n = 2 + len(_PARAM_NAMES)
    p = dict(zip(_PARAM_NAMES, refs[2:n]))
    w_in_tiles = refs[n:n + N_IN_TILES]
    w_out_tiles = refs[n + N_IN_TILES:n + N_IN_TILES + N_OUT_TILES]
    n += N_IN_TILES + N_OUT_TILES
    o_ref = refs[n]
    scratch = refs[n + 1:]
    proj_bufs = scratch[0:PROJ_SLOTS]
    y_bufs = scratch[PROJ_SLOTS:PROJ_SLOTS + STEP_CHUNKS]
    hn_buf = scratch[PROJ_SLOTS + STEP_CHUNKS]
    s = dict(zip(_STATE_NAMES, scratch[PROJ_SLOTS + STEP_CHUNKS + 1:]))
    k = pl.program_id(0)

    @pl.when(k == 0)
    def _init():
        for ref in scratch:
            ref[...] = jnp.zeros(ref.shape, ref.dtype)

    def chunk_slot(j, project, mix_it):
        rows = slice(j * TOK, (j + 1) * TOK)
        proj_ref = proj_bufs[j % PROJ_SLOTS]

        def emit_out(cols, t):
            o_ref[rows, cols] = xlag_ref[rows, cols] + t

        def emit_proj(cols, t):
            proj_ref[:, cols] = t

        mxu_work = _matmul_tiles(y_bufs[j], w_out_tiles, emit_out)
        if project:
            mxu_work += _matmul_tiles(hn_buf, w_in_tiles, emit_proj)
        mix = iter(())
        if mix_it:
            cm = lax.rem(STEP_CHUNKS * k + j + chunks_per_seq - 1, chunks_per_seq)
            mix = _mix_pieces(proj_bufs[(j - 1) % PROJ_SLOTS], y_bufs[(j - 1) % STEP_CHUNKS], cm, p, s)

        for t, (width, thunk) in enumerate(mxu_work):
            for part in range(width // MXU_COLS):
                if t == 0 and part == 0 and project:
                    hn_buf[...] = _rms_scale(x_ref[rows, :]).astype(_BF16)
                else:
                    next(mix, None)
            thunk()
            if is_last and t == N_OUT_TILES - 1:
                o_ref[rows, :] = _rmsnorm(o_ref[rows, :], _vec(p["vec"], "final_g"))
        for _ in mix:
            pass

    last_step = pl.num_programs(0) - 1

    @pl.when(k < last_step)
    def _steady():
        for j in range(STEP_CHUNKS):
            chunk_slot(j, project=True, mix_it=True)

    @pl.when(k == last_step)
    def _drain():
        for j in range(STEP_CHUNKS):
            chunk_slot(j, project=False, mix_it=(j == 0))


def _block_diag(w):
    n, d = w.shape[-3], w.shape[-2]
    eye = jnp.eye(n, dtype=w.dtype)
    out = eye[:, None, :, None] * w[..., :, :, None, :]
    return out.reshape(w.shape[:-3] + (n * d, n * d))


def _prepare_params(norm_g, w_in, sgu_norm_g, sgu_w, sgu_b, pool_w, pool_b, pool_scale, conv_w, conv_b,
                    lru_wa, lru_ba, lru_wx, lru_bx, lru_lambda, branch_norm_g, w_out, final_g):
    depth = norm_g.shape[0]
    f32 = lambda a: a.astype(_F32)
    hph = C_HEADS // 2
    both = jnp.stack([f32(lru_wa), f32(lru_wx)], axis=1)
    both = both.reshape(depth, 2, 2, hph, C_HEAD_DIM, C_HEAD_DIM)
    gate_w = jnp.einsum("lpdgio,gh->ldgipho", both, jnp.eye(hph, dtype=_F32))
    gate_w = gate_w.reshape(depth, 2, C_HALF, 2 * C_HALF)
    half_gate_cols = jnp.concatenate([
        jnp.ones((2 * A_WIDTH,)), jnp.full((A_WIDTH,), 0.5), jnp.ones((B_WIDTH,)), jnp.full((B_WIDTH,), 0.5),
        jnp.ones((C_WIDTH,)), jnp.full((C_WIDTH,), 0.5)]).astype(_F32)
    sgu_bias = jnp.repeat(jnp.swapaxes(sgu_b, 1, 2), A_HEAD_DIM, axis=2).astype(_F32)

    def vec_row(*parts):
        row = jnp.concatenate([f32(a) for a in parts], axis=-1)
        return jnp.pad(row, ((0, 0), (0, D_MODEL - row.shape[-1])))

    vec = jnp.stack(
        [vec_row(jnp.broadcast_to(final_g, (depth, D_MODEL))),
         vec_row(sgu_norm_g, pool_b, pool_scale),
         vec_row(0.5 * conv_b, lru_lambda),
         vec_row(0.5 * lru_ba, 0.5 * lru_bx)]
        + [vec_row(0.5 * conv_w[:, k]) for k in range(CONV_WIDTH)], axis=1)
    small = dict(vec=vec, sgu_w=f32(sgu_w), sgu_bias=sgu_bias, pool_w=_block_diag(pool_w).astype(_BF16),
                 gate_w=gate_w.astype(_BF16))
    w_in_bf = (f32(w_in) * f32(norm_g)[:, :, None] * half_gate_cols).astype(_BF16)
    w_out_bf = (f32(w_out) * f32(branch_norm_g)[:, :, None]).astype(_BF16)
    return small, w_in_bf, w_out_bf


def _layer(x, small, w_in_bf, w_out_bf, layer, *, is_last, name):
    bsz, seq, _ = x.shape
    step_rows = STEP_CHUNKS * TOK
    assert seq % step_rows == 0 and TOK % SGU_BLOCK == 0 and STEP_CHUNKS % PROJ_SLOTS == 0
    steps_per_seq = seq // step_rows
    n_real = bsz * steps_per_seq
    small_ops = tuple(small[n] for n in _PARAM_NAMES)
    operands = (x, x) + small_ops + (w_in_bf,) * N_IN_TILES + (w_out_bf,) * N_OUT_TILES

    def x_map(k):
        kk = jnp.minimum(k, n_real - 1)
        return (kk // steps_per_seq, kk % steps_per_seq, 0)

    def lag_map(k):
        kl = jnp.maximum(k - LAG_STEPS, 0)
        return (kl // steps_per_seq, kl % steps_per_seq, 0)

    def of_layer(shape):
        return pl.BlockSpec((None,) + tuple(shape[1:]), lambda k: (layer,) + (0,) * (len(shape) - 1),
                            pipeline_mode=pl.Buffered(1))

    in_specs = [pl.BlockSpec((None, step_rows, D_MODEL), x_map),
                pl.BlockSpec((None, step_rows, D_MODEL), lag_map)]
    in_specs += [of_layer(op.shape) for op in small_ops]
    for w, widths in ((w_in_bf, IN_TILE_WIDTHS), (w_out_bf, OUT_TILE_WIDTHS)):
        c0 = 0
        for width in widths:
            assert c0 % width == 0
            in_specs.append(pl.BlockSpec((None, w.shape[1], width),
                                         functools.partial(lambda k, ci: (layer, 0, ci), ci=c0 // width),
                                         pipeline_mode=pl.Buffered(1)))
            c0 += width
    scratch_shapes = (
        [pltpu.VMEM((TOK, D_IN), _F32) for _ in range(PROJ_SLOTS)]
        + [pltpu.VMEM((TOK, D_MODEL), _BF16) for _ in range(STEP_CHUNKS)]
        + [pltpu.VMEM((TOK, D_MODEL), _BF16),
           pltpu.VMEM((TOK + POOL_HALO, B_WIDTH), _F32),
           pltpu.VMEM((TOK + POOL_HALO, B_WIDTH), _F32),
           pltpu.VMEM((TOK + POOL_HALO, LANES), _F32),
           pltpu.VMEM((TOK + POOL_HALO, LANES), _F32),
           pltpu.VMEM((CONV_HALO, C_WIDTH), _F32),
           pltpu.VMEM((SUBLANES, C_WIDTH), _F32),
           pltpu.VMEM((C_WIDTH // LANES, SEGS * SEG_PITCH, LANES), _F32),
           pltpu.VMEM((C_WIDTH // LANES, SEGS * SEG_PITCH, LANES), _F32)])
    return pl.pallas_call(
        functools.partial(_layer_kernel, is_last=is_last, chunks_per_seq=seq // TOK),
        out_shape=jax.ShapeDtypeStruct(x.shape, x.dtype),
        grid=(n_real + LAG_STEPS,),
        in_specs=in_specs,
        out_specs=pl.BlockSpec((None, step_rows, D_MODEL), lag_map),
        scratch_shapes=scratch_shapes,
        compiler_params=pltpu.CompilerParams(
            dimension_semantics=("arbitrary",),
            vmem_limit_bytes=VMEM_LIMIT_BYTES,
        ),
        name=name,
    )(*operands)


def kernel(x, norm_g, w_in, sgu_norm_g, sgu_w, sgu_b, pool_w, pool_b, pool_scale, conv_w, conv_b,
           lru_wa, lru_ba, lru_wx, lru_bx, lru_lambda, branch_norm_g, w_out, final_g):
    small, w_in_bf, w_out_bf = _prepare_params(
        norm_g, w_in, sgu_norm_g, sgu_w, sgu_b, pool_w, pool_b, pool_scale, conv_w, conv_b,
        lru_wa, lru_ba, lru_wx, lru_bx, lru_lambda, branch_norm_g, w_out, final_g)
    for l in range(DEPTH):
        x = _layer(x, small, w_in_bf, w_out_bf, l, is_last=(l == DEPTH - 1), name=f"hybrid_layer{l}")
    return x
```

```python
import functools

import jax
import jax.numpy as jnp
from jax import lax
from jax.experimental import pallas as pl
from jax.experimental.pallas import tpu as pltpu

D_MODEL = 1024
DEPTH = 2
CHUNK = 64
EPS = 1e-6
A_WIDTH = 256
A_HEADS = 4
A_HEAD_DIM = 64
SGU_BLOCK = 128
B_WIDTH = 256
POOL_WINDOWS = (2, 4, 8, 16)
B_GROUP_DIM = 64
C_WIDTH = 512
C_HEADS = 8
C_HEAD_DIM = 64
CONV_WIDTH = 4
LRU_C = 8.0
D_IN = 3 * A_WIDTH + 2 * B_WIDTH + 2 * C_WIDTH
OFF_B = 3 * A_WIDTH
OFF_C = OFF_B + 2 * B_WIDTH

SUBLANES = 8
LANES = 128
MXU_COLS = 256
TOK = 256
STEP_CHUNKS = 4
PROJ_SLOTS = 2
LAG_STEPS = 1
POOL_HALO = 16
CONV_HALO = 8
C_HALF = C_WIDTH // 2
SEGS = SUBLANES
SEG_ROWS = TOK // SEGS
SEG_PITCH = SEG_ROWS + CONV_HALO
VMEM_LIMIT_BYTES = 56 * 1024 * 1024

_BF16 = jnp.bfloat16
_F32 = jnp.float32
F32_MAX = float(jnp.finfo(jnp.float32).max)
NEG_LOG2_E = -1.4426950408889634


def _rms_scale(x):
    ms = jnp.mean(x * x, axis=-1, keepdims=True)
    return x * lax.rsqrt(ms + EPS)


def _rmsnorm(x, g):
    return _rms_scale(x) * g


VEC_ROWS = 8
_VEC_SLOTS = {
    "final_g": (0, 0, D_MODEL),
    "sgu_g": (1, 0, A_WIDTH), "pool_bias": (1, A_WIDTH, B_WIDTH),
    "half_conv_b": (2, 0, C_WIDTH), "lam": (2, C_WIDTH, C_WIDTH),
    "half_b_a": (3, 0, C_WIDTH), "half_b_x": (3, C_WIDTH, C_WIDTH),
}
VEC_CONV_W_ROW = 4


def _vec(vec_ref, name):
    r, c0, width = _VEC_SLOTS[name]
    return vec_ref[r:r + 1, c0:c0 + width]


def _gated_bf16(u, h):
    hb = h.astype(_BF16)
    return u.astype(_BF16) * (hb * jnp.tanh(hb) + hb)


def _seg_store(seg_ref, x_ref, col0, width, halo):
    for slab in range(width // LANES):
        lanes = slice(col0 + slab * LANES, col0 + (slab + 1) * LANES)
        for g in range(SEGS):
            base = g * SEG_PITCH
            if g == 0:
                before = halo[:, slab * LANES:(slab + 1) * LANES]
            else:
                before = x_ref[g * SEG_ROWS - CONV_HALO:g * SEG_ROWS, lanes]
            seg_ref[slab, base:base + CONV_HALO, :] = before
            seg_ref[slab, base + CONV_HALO:base + SEG_PITCH, :] = x_ref[g * SEG_ROWS:(g + 1) * SEG_ROWS, lanes]


def _seg_row(seg_ref, r, slabs):
    return jnp.concatenate([seg_ref[slab, pl.ds(r, SEGS, stride=SEG_PITCH), :] for slab in slabs], axis=1)


def _segment_scan(a, b, h0, seg_ref, slabs):
    width = a.shape[-1]
    a3 = a.reshape(SEG_ROWS, SEGS, width)
    b3 = b.reshape(SEG_ROWS, SEGS, width)
    acum, hloc = [a3[0]], [b3[0]]
    for j in range(1, SEG_ROWS):
        acum.append(acum[-1] * a3[j])
        hloc.append(a3[j] * hloc[-1] + b3[j])
    end_a, end_h = acum[-1], hloc[-1]
    sub = lax.broadcasted_iota(jnp.int32, (SEGS, width), 0)
    carry = h0
    for _ in range(SEGS - 1):
        carry = jnp.where(sub == 0, h0, pltpu.roll(end_h + end_a * carry, 1, axis=0))
    for j in range(SEG_ROWS):
        hj = hloc[j] + acum[j] * carry
        for n, slab in enumerate(slabs):
            seg_ref[slab, pl.ds(j, SEGS, stride=SEG_PITCH), :] = hj[:, n * LANES:(n + 1) * LANES]
    last = end_h + end_a * carry
    return jnp.broadcast_to(last[SEGS - 1:SEGS, :], (SEGS, width))


def _seg_load_time_order(seg_ref, slabs):
    return jnp.concatenate(
        [jnp.concatenate([seg_ref[slab, g * SEG_PITCH:g * SEG_PITCH + SEG_ROWS, :] for g in range(SEGS)], axis=0)
         for slab in slabs], axis=1)


def _mix_pieces(proj_ref, y_ref, cm, p, s):
    first = cm == 0
    vec = p["vec"]

    for name in ("xb_ext", "s2_ext", "s4_ext", "s8_ext"):
        ref = s[name]
        ref[0:POOL_HALO, :] = jnp.where(first, 0.0, ref[0:POOL_HALO, :])
    cx_halo = jnp.where(first, 0.0, s["cx_halo"][...])
    hc = jnp.where(first, 0.0, s["h_carry"][...])

    v = _rmsnorm(proj_ref[:, A_WIDTH:2 * A_WIDTH], _vec(vec, "sgu_g")).astype(_BF16)
    ci = lax.broadcasted_iota(jnp.int32, (SGU_BLOCK, SGU_BLOCK), 0) // CHUNK
    cj = lax.broadcasted_iota(jnp.int32, (SGU_BLOCK, SGU_BLOCK), 1) // CHUNK
    mask = (cj <= ci).astype(_F32)
    wm = jnp.concatenate([(p["sgu_w"][h] * mask).astype(_BF16) for h in range(A_HEADS)], axis=1)
    head_of_lane = lax.broadcasted_iota(jnp.int32, (1, A_WIDTH), 1) // A_HEAD_DIM
    ys = []
    for n in range(TOK // SGU_BLOCK):
        vn = v[n * SGU_BLOCK:(n + 1) * SGU_BLOCK, :]
        rhs = jnp.concatenate([jnp.where(head_of_lane == h, vn, jnp.zeros_like(vn)) for h in range(A_HEADS)], axis=0)
        ys.append(jnp.dot(wm, rhs, preferred_element_type=_F32) + p["sgu_bias"][...])
    ya = proj_ref[:, 0:A_WIDTH] * jnp.concatenate(ys, axis=0)
    yield
    y_ref[:, 0:A_WIDTH] = _gated_bf16(_rms_scale(ya), proj_ref[:, 2 * A_WIDTH:3 * A_WIDTH])
    yield

    xb_ext, s2_ext, s4_ext, s8_ext = s["xb_ext"], s["s2_ext"], s["s4_ext"], s["s8_ext"]
    bx = proj_ref[:, OFF_B:OFF_B + B_WIDTH]
    xb_ext[POOL_HALO:POOL_HALO + TOK, :] = bx
    s2 = bx + xb_ext[pl.ds(POOL_HALO - 1, TOK), :]
    s2_ext[POOL_HALO:POOL_HALO + TOK, :] = s2
    s4 = s2 + s2_ext[pl.ds(POOL_HALO - 2, TOK), :]
    s4_hi = s4[:, LANES:]
    s4_ext[POOL_HALO:POOL_HALO + TOK, :] = s4_hi
    s8 = s4_hi + s4_ext[pl.ds(POOL_HALO - 4, TOK), :]
    s8_ext[POOL_HALO:POOL_HALO + TOK, :] = s8
    s16 = s8 + s8_ext[pl.ds(POOL_HALO - 8, TOK), :]
    lane = lax.broadcasted_iota(jnp.int32, (1, LANES), 1)
    wsum = jnp.concatenate([jnp.where(lane < B_GROUP_DIM, s2[:, :LANES], s4[:, :LANES]),
                            jnp.where(lane < B_GROUP_DIM, s8, s16)], axis=1)
    lane_b = lax.broadcasted_iota(jnp.int32, (1, B_WIDTH), 1) // B_GROUP_DIM
    win = jnp.where(lane_b == 0, POOL_WINDOWS[0],
                    jnp.where(lane_b == 1, POOL_WINDOWS[1],
                              jnp.where(lane_b == 2, POOL_WINDOWS[2], POOL_WINDOWS[3])))
    inv_win = 1.0 / win.astype(_F32)
    t_head = cm * TOK + lax.broadcasted_iota(jnp.int32, (POOL_HALO, B_WIDTH), 0)
    cnt_head = jnp.minimum(t_head + 1, win).astype(_F32)
    mean = jnp.concatenate([wsum[0:POOL_HALO] / cnt_head, wsum[POOL_HALO:] * inv_win], axis=0)
    pooled = (mean - bx).astype(_BF16)
    for ref in (xb_ext, s2_ext, s4_ext, s8_ext):
        ref[0:POOL_HALO, :] = ref[TOK:TOK + POOL_HALO, :]
    yield
    yb = _rms_scale(jnp.dot(pooled, p["pool_w"][...], preferred_element_type=_F32) + _vec(vec, "pool_bias"))
    y_ref[:, A_WIDTH:A_WIDTH + B_WIDTH] = _gated_bf16(yb, proj_ref[:, OFF_B + B_WIDTH:OFF_B + 2 * B_WIDTH])
    yield

    cw = vec[VEC_CONV_W_ROW:VEC_CONV_W_ROW + CONV_WIDTH, 0:C_WIDTH]
    half_c_softplus = (0.5 * LRU_C) * jax.nn.softplus(-_vec(vec, "lam"))
    seg_in, seg_out = s["seg_in"], s["seg_out"]
    all_slabs = tuple(range(C_WIDTH // LANES))
    _seg_store(seg_in, proj_ref, OFF_C, C_WIDTH, cx_halo)
    s["cx_halo"][...] = proj_ref[TOK - CONV_HALO:TOK, OFF_C:OFF_C + C_WIDTH]
    taps = [jnp.broadcast_to(cw[k:k + 1, :], (SEGS, C_WIDTH)) for k in range(CONV_WIDTH)]
    cbias = jnp.broadcast_to(_vec(vec, "half_conv_b"), (SEGS, C_WIDTH))
    xrow = {r: _seg_row(seg_in, r, all_slabs) for r in range(CONV_HALO - (CONV_WIDTH - 1), SEG_PITCH)}
    conv_rows = []
    for j in range(SEG_ROWS):
        acc = cbias + xrow[j + CONV_HALO] * taps[CONV_WIDTH - 1]
        for k in range(CONV_WIDTH - 1):
            acc = acc + xrow[j + CONV_HALO - (CONV_WIDTH - 1 - k)] * taps[k]
        conv_rows.append(acc)
    half_conv_all = jnp.concatenate(conv_rows, axis=0)
    yield
    for d in range(2):
        lo = d * C_HALF
        half_conv = half_conv_all[:, lo:lo + C_HALF]
        half_pre = jnp.dot(half_conv.astype(_BF16), p["gate_w"][d], preferred_element_type=_F32)
        t_r = jnp.tanh(half_pre[:, 0:C_HALF] + _vec(vec, "half_b_a")[:, lo:lo + C_HALF])
        t_i = jnp.tanh(half_pre[:, C_HALF:] + _vec(vec, "half_b_x")[:, lo:lo + C_HALF])
        half_csp = half_c_softplus[:, lo:lo + C_HALF]
        neg_log_a = half_csp * t_r + half_csp
        a = jnp.exp2(neg_log_a * NEG_LOG2_E)
        msq = jnp.tanh(neg_log_a) * (a * a + 1.0)
        mult = msq * jnp.minimum(lax.rsqrt(msq), F32_MAX)
        bterm = mult * (half_conv * t_i + half_conv)
        yield
        slabs = all_slabs[d * (C_HALF // LANES):(d + 1) * (C_HALF // LANES)]
        s["h_carry"][:, lo:lo + C_HALF] = _segment_scan(a, bterm, hc[:, lo:lo + C_HALF], seg_out, slabs)
        yield
    yc = _rms_scale(_seg_load_time_order(seg_out, all_slabs))
    y_ref[:, A_WIDTH + B_WIDTH:] = _gated_bf16(yc, proj_ref[:, OFF_C + C_WIDTH:OFF_C + 2 * C_WIDTH])
    yield


def _matmul_tiles(lhs_ref, w_tile_refs, emit):
    def tile(c0, ref):
        cols = slice(c0, c0 + ref.shape[-1])
        emit(cols, jnp.dot(lhs_ref[...], ref[...], preferred_element_type=_F32))
    out, c0 = [], 0
    for ref in w_tile_refs:
        out.append((ref.shape[-1], functools.partial(tile, c0, ref)))
        c0 += ref.shape[-1]
    return out


_PARAM_NAMES = ("vec", "sgu_w", "sgu_bias", "pool_w", "gate_w")
TILE_COLS = 2 * MXU_COLS
IN_TILE_WIDTHS = (TILE_COLS,) * (D_IN // TILE_COLS) + ((D_IN % TILE_COLS,) if D_IN % TILE_COLS else ())
OUT_TILE_WIDTHS = (TILE_COLS,) * (D_MODEL // TILE_COLS)
N_IN_TILES = len(IN_TILE_WIDTHS)
N_OUT_TILES = len(OUT_TILE_WIDTHS)
_STATE_NAMES = ("xb_ext", "s2_ext", "s4_ext", "s8_ext", "cx_halo", "h_carry", "seg_in", "seg_out")


def _layer_kernel(*refs, is_last, chunks_per_seq):
    x_ref, xlag_ref = refs[0], refs[1]
    n = 2 + len(_PARAM_NAMES)
    p = dict(zip(_PARAM_NAMES, refs[2:n]))
    w_in_tiles = refs[n:n + N_IN_TILES]
    w_out_tiles = refs[n + N_IN_TILES:n + N_IN_TILES + N_OUT_TILES]
    n += N_IN_TILES + N_OUT_TILES
    o_ref = refs[n]
    scratch = refs[n + 1:]
    proj_bufs = scratch[0:PROJ_SLOTS]
    y_bufs = scratch[PROJ_SLOTS:PROJ_SLOTS + STEP_CHUNKS]
    hn_buf = scratch[PROJ_SLOTS + STEP_CHUNKS]
    s = dict(zip(_STATE_NAMES, scratch[PROJ_SLOTS + STEP_CHUNKS + 1:]))
    k = pl.program_id(0)

    @pl.when(k == 0)
    def _init():
        for ref in scratch:
            ref[...] = jnp.zeros(ref.shape, ref.dtype)

    def chunk_slot(j, project, mix_it):
        rows = slice(j * TOK, (j + 1) * TOK)
        proj_ref = proj_bufs[j % PROJ_SLOTS]

        def emit_out(cols, t):
            o_ref[rows, cols] = xlag_ref[rows, cols] + t

        def emit_proj(cols, t):
            proj_ref[:, cols] = t

        mxu_work = _matmul_tiles(y_bufs[j], w_out_tiles, emit_out)
        if project:
            mxu_work += _matmul_tiles(hn_buf, w_in_tiles, emit_proj)
        mix = iter(())
        if mix_it:
            cm = lax.rem(STEP_CHUNKS * k + j + chunks_per_seq - 1, chunks_per_seq)
            mix = _mix_pieces(proj_bufs[(j - 1) % PROJ_SLOTS], y_bufs[(j - 1) % STEP_CHUNKS], cm, p, s)

        for t, (width, thunk) in enumerate(mxu_work):
            for part in range(width // MXU_COLS):
                if t == 0 and part == 0 and project:
                    hn_buf[...] = _rms_scale(x_ref[rows, :]).astype(_BF16)
                else:
                    next(mix, None)
            thunk()
            if is_last and t == N_OUT_TILES - 1:
                o_ref[rows, :] = _rmsnorm(o_ref[rows, :], _vec(p["vec"], "final_g"))
        for _ in mix:
            pass

    last_step = pl.num_programs(0) - 1

    @pl.when(k < last_step)
    def _steady():
        for j in range(STEP_CHUNKS):
            chunk_slot(j, project=True, mix_it=True)

    @pl.when(k == last_step)
    def _drain():
        for j in range(STEP_CHUNKS):
            chunk_slot(j, project=False, mix_it=(j == 0))


def _block_diag(w):
    n, d = w.shape[-3], w.shape[-2]
    eye = jnp.eye(n, dtype=w.dtype)
    out = eye[:, None, :, None] * w[..., :, :, None, :]
    return out.reshape(w.shape[:-3] + (n * d, n * d))


def _prepare_params(norm_g, w_in, sgu_norm_g, sgu_w, sgu_b, pool_w, pool_b, pool_scale, conv_w, conv_b,
                    lru_wa, lru_ba, lru_wx, lru_bx, lru_lambda, branch_norm_g, w_out, final_g):
    depth = norm_g.shape[0]
    f32 = lambda a: a.astype(_F32)
    hph = C_HEADS // 2
    both = jnp.stack([f32(lru_wa), f32(lru_wx)], axis=1)
    both = both.reshape(depth, 2, 2, hph, C_HEAD_DIM, C_HEAD_DIM)
    gate_w = jnp.einsum("lpdgio,gh->ldgipho", both, jnp.eye(hph, dtype=_F32))
    gate_w = gate_w.reshape(depth, 2, C_HALF, 2 * C_HALF)
    half_gate_cols = jnp.concatenate([
        jnp.ones((2 * A_WIDTH,)), jnp.full((A_WIDTH,), 0.5), jnp.ones((B_WIDTH,)), jnp.full((B_WIDTH,), 0.5),
        jnp.ones((C_WIDTH,)), jnp.full((C_WIDTH,), 0.5)]).astype(_F32)
    sgu_bias = jnp.repeat(jnp.swapaxes(sgu_b, 1, 2), A_HEAD_DIM, axis=2).astype(_F32)

    def vec_row(*parts):
        row = jnp.concatenate([f32(a) for a in parts], axis=-1)
        return jnp.pad(row, ((0, 0), (0, D_MODEL - row.shape[-1])))

    vec = jnp.stack(
        [vec_row(jnp.broadcast_to(final_g, (depth, D_MODEL))),
         vec_row(sgu_norm_g, pool_b * pool_scale),
         vec_row(0.5 * conv_b, lru_lambda),
         vec_row(0.5 * lru_ba, 0.5 * lru_bx)]
        + [vec_row(0.5 * conv_w[:, k]) for k in range(CONV_WIDTH)], axis=1)
    pool_w_scaled = _block_diag(f32(pool_w)) * f32(pool_scale)[:, None, :]
    small = dict(vec=vec, sgu_w=f32(sgu_w), sgu_bias=sgu_bias, pool_w=pool_w_scaled.astype(_BF16),
                 gate_w=gate_w.astype(_BF16))
    w_in_bf = (f32(w_in) * f32(norm_g)[:, :, None] * half_gate_cols).astype(_BF16)
    w_out_bf = (f32(w_out) * f32(branch_norm_g)[:, :, None]).astype(_BF16)
    return small, w_in_bf, w_out_bf


def _layer(x, small, w_in_bf, w_out_bf, layer, *, is_last, name):
    bsz, seq, _ = x.shape
    step_rows = STEP_CHUNKS * TOK
    assert seq % step_rows == 0 and TOK % SGU_BLOCK == 0 and STEP_CHUNKS % PROJ_SLOTS == 0
    steps_per_seq = seq // step_rows
    n_real = bsz * steps_per_seq
    small_ops = tuple(small[n] for n in _PARAM_NAMES)
    operands = (x, x) + small_ops + (w_in_bf,) * N_IN_TILES + (w_out_bf,) * N_OUT_TILES

    def x_map(k):
        kk = jnp.minimum(k, n_real - 1)
        return (kk // steps_per_seq, kk % steps_per_seq, 0)

    def lag_map(k):
        kl = jnp.maximum(k - LAG_STEPS, 0)
        return (kl // steps_per_seq, kl % steps_per_seq, 0)

    def of_layer(shape):
        return pl.BlockSpec((None,) + tuple(shape[1:]), lambda k: (layer,) + (0,) * (len(shape) - 1),
                            pipeline_mode=pl.Buffered(1))

    in_specs = [pl.BlockSpec((None, step_rows, D_MODEL), x_map),
                pl.BlockSpec((None, step_rows, D_MODEL), lag_map)]
    in_specs += [of_layer(op.shape) for op in small_ops]
    for w, widths in ((w_in_bf, IN_TILE_WIDTHS), (w_out_bf, OUT_TILE_WIDTHS)):
        c0 = 0
        for width in widths:
            assert c0 % width == 0
            in_specs.append(pl.BlockSpec((None, w.shape[1], width),
                                         functools.partial(lambda k, ci: (layer, 0, ci), ci=c0 // width),
                                         pipeline_mode=pl.Buffered(1)))
            c0 += width
    scratch_shapes = (
        [pltpu.VMEM((TOK, D_IN), _F32) for _ in range(PROJ_SLOTS)]
        + [pltpu.VMEM((TOK, D_MODEL), _BF16) for _ in range(STEP_CHUNKS)]
        + [pltpu.VMEM((TOK, D_MODEL), _BF16),
           pltpu.VMEM((TOK + POOL_HALO, B_WIDTH), _F32),
           pltpu.VMEM((TOK + POOL_HALO, B_WIDTH), _F32),
           pltpu.VMEM((TOK + POOL_HALO, LANES), _F32),
           pltpu.VMEM((TOK + POOL_HALO, LANES), _F32),
           pltpu.VMEM((CONV_HALO, C_WIDTH), _F32),
           pltpu.VMEM((SUBLANES, C_WIDTH), _F32),
           pltpu.VMEM((C_WIDTH // LANES, SEGS * SEG_PITCH, LANES), _F32),
           pltpu.VMEM((C_WIDTH // LANES, SEGS * SEG_PITCH, LANES), _F32)])
    return pl.pallas_call(
        functools.partial(_layer_kernel, is_last=is_last, chunks_per_seq=seq // TOK),
        out_shape=jax.ShapeDtypeStruct(x.shape, x.dtype),
        grid=(n_real + LAG_STEPS,),
        in_specs=in_specs,
        out_specs=pl.BlockSpec((None, step_rows, D_MODEL), lag_map),
        scratch_shapes=scratch_shapes,
        compiler_params=pltpu.CompilerParams(
            dimension_semantics=("arbitrary",),
            vmem_limit_bytes=VMEM_LIMIT_BYTES,
        ),
        name=name,
    )(*operands)


def kernel(x, norm_g, w_in, sgu_norm_g, sgu_w, sgu_b, pool_w, pool_b, pool_scale, conv_w, conv_b,
           lru_wa, lru_ba, lru_wx, lru_bx, lru_lambda, branch_norm_g, w_out, final_g):
    small, w_in_bf, w_out_bf = _prepare_params(
        norm_g, w_in, sgu_norm_g, sgu_w, sgu_b, pool_w, pool_b, pool_scale, conv_w, conv_b,
        lru_wa, lru_ba, lru_wx, lru_bx, lru_lambda, branch_norm_g, w_out, final_g)
    for l in range(DEPTH):
        x = _layer(x, small, w_in_bf, w_out_bf, l, is_last=(l == DEPTH - 1), name=f"hybrid_layer{l}")
    return x
```

```python
import functools

import jax
import jax.numpy as jnp
from jax import lax
from jax.experimental import pallas as pl
from jax.experimental.pallas import tpu as pltpu

D_MODEL = 1024
DEPTH = 2
CHUNK = 64
EPS = 1e-6
A_WIDTH = 256
A_HEADS = 4
A_HEAD_DIM = 64
SGU_BLOCK = 128
B_WIDTH = 256
POOL_WINDOWS = (2, 4, 8, 16)
B_GROUP_DIM = 64
C_WIDTH = 512
C_HEADS = 8
C_HEAD_DIM = 64
CONV_WIDTH = 4
LRU_C = 8.0
D_IN = 3 * A_WIDTH + 2 * B_WIDTH + 2 * C_WIDTH
OFF_B = 3 * A_WIDTH
OFF_C = OFF_B + 2 * B_WIDTH

SUBLANES = 8
LANES = 128
MXU_COLS = 256
TOK = 256
STEP_CHUNKS = 4
PROJ_SLOTS = 2
LAG_STEPS = 1
POOL_HALO = 16
CONV_HALO = 8
C_HALF = C_WIDTH // 2
SEGS = SUBLANES
SEG_ROWS = TOK // SEGS
SEG_PITCH = SEG_ROWS + CONV_HALO
VMEM_LIMIT_BYTES = 56 * 1024 * 1024

_BF16 = jnp.bfloat16
_F32 = jnp.float32
F32_MAX = float(jnp.finfo(jnp.float32).max)
NEG_LOG2_E = -1.4426950408889634


def _rms_scale(x):
    ms = jnp.mean(x * x, axis=-1, keepdims=True)
    return x * lax.rsqrt(ms + EPS)


def _rmsnorm(x, g):
    return _rms_scale(x) * g


VEC_ROWS = 8
_VEC_SLOTS = {
    "final_g": (0, 0, D_MODEL),
    "sgu_g": (1, 0, A_WIDTH), "pool_bias": (1, A_WIDTH, B_WIDTH),
    "half_conv_b": (2, 0, C_WIDTH), "lam": (2, C_WIDTH, C_WIDTH),
    "half_b_a": (3, 0, C_WIDTH), "half_b_x": (3, C_WIDTH, C_WIDTH),
}
VEC_CONV_W_ROW = 4


def _vec(vec_ref, name):
    r, c0, width = _VEC_SLOTS[name]
    return vec_ref[r:r + 1, c0:c0 + width]


def _gated_bf16(u, h):
    hb = h.astype(_BF16)
    return u.astype(_BF16) * (hb * jnp.tanh(hb) + hb)


def _seg_store(seg_ref, x_ref, col0, width, halo):
    for slab in range(width // LANES):
        lanes = slice(col0 + slab * LANES, col0 + (slab + 1) * LANES)
        for g in range(SEGS):
            base = g * SEG_PITCH
            if g == 0:
                before = halo[:, slab * LANES:(slab + 1) * LANES]
            else:
                before = x_ref[g * SEG_ROWS - CONV_HALO:g * SEG_ROWS, lanes]
            seg_ref[slab, base:base + CONV_HALO, :] = before
            seg_ref[slab, base + CONV_HALO:base + SEG_PITCH, :] = x_ref[g * SEG_ROWS:(g + 1) * SEG_ROWS, lanes]


def _seg_row(seg_ref, r, slabs):
    return jnp.concatenate([seg_ref[slab, pl.ds(r, SEGS, stride=SEG_PITCH), :] for slab in slabs], axis=1)


def _segment_scan(a, b, h0, seg_ref, slabs):
    width = a.shape[-1]
    a3 = a.reshape(SEG_ROWS, SEGS, width)
    b3 = b.reshape(SEG_ROWS, SEGS, width)
    acum, hloc = [a3[0]], [b3[0]]
    for j in range(1, SEG_ROWS):
        acum.append(acum[-1] * a3[j])
        hloc.append(a3[j] * hloc[-1] + b3[j])
    end_a, end_h = acum[-1], hloc[-1]
    sub = lax.broadcasted_iota(jnp.int32, (SEGS, width), 0)
    carry = h0
    for _ in range(SEGS - 1):
        carry = jnp.where(sub == 0, h0, pltpu.roll(end_h + end_a * carry, 1, axis=0))
    for j in range(SEG_ROWS):
        hj = hloc[j] + acum[j] * carry
        for n, slab in enumerate(slabs):
            seg_ref[slab, pl.ds(j, SEGS, stride=SEG_PITCH), :] = hj[:, n * LANES:(n + 1) * LANES]
    last = end_h + end_a * carry
    return jnp.broadcast_to(last[SEGS - 1:SEGS, :], (SEGS, width))


def _seg_load_time_order(seg_ref, slabs):
    return jnp.concatenate(
        [jnp.concatenate([seg_ref[slab, g * SEG_PITCH:g * SEG_PITCH + SEG_ROWS, :] for g in range(SEGS)], axis=0)
         for slab in slabs], axis=1)


def _mix_pieces(proj_ref, y_ref, cm, p, s):
    first = cm == 0
    vec = p["vec"]

    for name in ("xb_ext", "s2_ext", "s4_ext", "s8_ext"):
        ref = s[name]
        ref[0:POOL_HALO, :] = jnp.where(first, 0.0, ref[0:POOL_HALO, :])
    cx_halo = jnp.where(first, 0.0, s["cx_halo"][...])
    hc = jnp.where(first, 0.0, s["h_carry"][...])

    v = _rmsnorm(proj_ref[:, A_WIDTH:2 * A_WIDTH], _vec(vec, "sgu_g")).astype(_BF16)
    ci = lax.broadcasted_iota(jnp.int32, (SGU_BLOCK, SGU_BLOCK), 0) // CHUNK
    cj = lax.broadcasted_iota(jnp.int32, (SGU_BLOCK, SGU_BLOCK), 1) // CHUNK
    mask = (cj <= ci).astype(_F32)
    wm = jnp.concatenate([(p["sgu_w"][h] * mask).astype(_BF16) for h in range(A_HEADS)], axis=1)
    head_of_lane = lax.broadcasted_iota(jnp.int32, (1, A_WIDTH), 1) // A_HEAD_DIM
    ys = []
    for n in range(TOK // SGU_BLOCK):
        vn = v[n * SGU_BLOCK:(n + 1) * SGU_BLOCK, :]
        rhs = jnp.concatenate([jnp.where(head_of_lane == h, vn, jnp.zeros_like(vn)) for h in range(A_HEADS)], axis=0)
        ys.append(jnp.dot(wm, rhs, preferred_element_type=_F32) + p["sgu_bias"][...])
    ya = proj_ref[:, 0:A_WIDTH] * jnp.concatenate(ys, axis=0)
    yield
    y_ref[:, 0:A_WIDTH] = _gated_bf16(_rms_scale(ya), proj_ref[:, 2 * A_WIDTH:3 * A_WIDTH])
    yield

    xb_ext, s2_ext, s4_ext, s8_ext = s["xb_ext"], s["s2_ext"], s["s4_ext"], s["s8_ext"]
    bx = proj_ref[:, OFF_B:OFF_B + B_WIDTH]
    xb_ext[POOL_HALO:POOL_HALO + TOK, :] = bx
    s2 = bx + xb_ext[pl.ds(POOL_HALO - 1, TOK), :]
    s2_ext[POOL_HALO:POOL_HALO + TOK, :] = s2
    s4 = s2 + s2_ext[pl.ds(POOL_HALO - 2, TOK), :]
    s4_hi = s4[:, LANES:]
    s4_ext[POOL_HALO:POOL_HALO + TOK, :] = s4_hi
    s8 = s4_hi + s4_ext[pl.ds(POOL_HALO - 4, TOK), :]
    s8_ext[POOL_HALO:POOL_HALO + TOK, :] = s8
    s16 = s8 + s8_ext[pl.ds(POOL_HALO - 8, TOK), :]
    lane = lax.broadcasted_iota(jnp.int32, (1, LANES), 1)
    wsum = jnp.concatenate([jnp.where(lane < B_GROUP_DIM, s2[:, :LANES], s4[:, :LANES]),
                            jnp.where(lane < B_GROUP_DIM, s8, s16)], axis=1)
    lane_b = lax.broadcasted_iota(jnp.int32, (1, B_WIDTH), 1) // B_GROUP_DIM
    win = jnp.where(lane_b == 0, POOL_WINDOWS[0],
                    jnp.where(lane_b == 1, POOL_WINDOWS[1],
                              jnp.where(lane_b == 2, POOL_WINDOWS[2], POOL_WINDOWS[3])))
    inv_win = 1.0 / win.astype(_F32)
    t_head = cm * TOK + lax.broadcasted_iota(jnp.int32, (POOL_HALO, B_WIDTH), 0)
    cnt_head = jnp.minimum(t_head + 1, win).astype(_F32)
    mean = jnp.concatenate([wsum[0:POOL_HALO] / cnt_head, wsum[POOL_HALO:] * inv_win], axis=0)
    pooled = (mean - bx).astype(_BF16)
    for ref in (xb_ext, s2_ext, s4_ext, s8_ext):
        ref[0:POOL_HALO, :] = ref[TOK:TOK + POOL_HALO, :]
    yield
    yb = _rms_scale(jnp.dot(pooled, p["pool_w"][...], preferred_element_type=_F32) + _vec(vec, "pool_bias"))
    y_ref[:, A_WIDTH:A_WIDTH + B_WIDTH] = _gated_bf16(yb, proj_ref[:, OFF_B + B_WIDTH:OFF_B + 2 * B_WIDTH])
    yield

    cw = vec[VEC_CONV_W_ROW:VEC_CONV_W_ROW + CONV_WIDTH, 0:C_WIDTH]
    half_c_softplus = (0.5 * LRU_C) * jax.nn.softplus(-_vec(vec, "lam"))
    seg_in, seg_out = s["seg_in"], s["seg_out"]
    all_slabs = tuple(range(C_WIDTH // LANES))
    _seg_store(seg_in, proj_ref, OFF_C, C_WIDTH, cx_halo)
    s["cx_halo"][...] = proj_ref[TOK - CONV_HALO:TOK, OFF_C:OFF_C + C_WIDTH]
    taps = [jnp.broadcast_to(cw[k:k + 1, :], (SEGS, C_WIDTH)) for k in range(CONV_WIDTH)]
    cbias = jnp.broadcast_to(_vec(vec, "half_conv_b"), (SEGS, C_WIDTH))
    xrow = {r: _seg_row(seg_in, r, all_slabs) for r in range(CONV_HALO - (CONV_WIDTH - 1), SEG_PITCH)}
    conv_rows = []
    for j in range(SEG_ROWS):
        acc = cbias + xrow[j + CONV_HALO] * taps[CONV_WIDTH - 1]
        for k in range(CONV_WIDTH - 1):
            acc = acc + xrow[j + CONV_HALO - (CONV_WIDTH - 1 - k)] * taps[k]
        conv_rows.append(acc)
    half_conv_all = jnp.concatenate(conv_rows, axis=0)
    yield
    for d in range(2):
        lo = d * C_HALF
        half_conv = half_conv_all[:, lo:lo + C_HALF]
        half_pre = jnp.dot(half_conv.astype(_BF16), p["gate_w"][d], preferred_element_type=_F32)
        t_r = jnp.tanh(half_pre[:, 0:C_HALF] + _vec(vec, "half_b_a")[:, lo:lo + C_HALF])
        t_i = jnp.tanh(half_pre[:, C_HALF:] + _vec(vec, "half_b_x")[:, lo:lo + C_HALF])
        half_csp = half_c_softplus[:, lo:lo + C_HALF]
        neg_log_a = half_csp * t_r + half_csp
        a = jnp.exp2(neg_log_a * NEG_LOG2_E)
        msq = jnp.tanh(neg_log_a) * (a * a + 1.0)
        mult = msq * jnp.minimum(lax.rsqrt(msq), F32_MAX)
        bterm = mult * (half_conv * t_i + half_conv)
        yield
        slabs = all_slabs[d * (C_HALF // LANES):(d + 1) * (C_HALF // LANES)]
        s["h_carry"][:, lo:lo + C_HALF] = _segment_scan(a, bterm, hc[:, lo:lo + C_HALF], seg_out, slabs)
        yield
    yc = _rms_scale(_seg_load_time_order(seg_out, all_slabs))
    y_ref[:, A_WIDTH + B_WIDTH:] = _gated_bf16(yc, proj_ref[:, OFF_C + C_WIDTH:OFF_C + 2 * C_WIDTH])
    yield


def _matmul_tiles(lhs_ref, w_tile_refs, emit):
    def tile(c0, ref):
        cols = slice(c0, c0 + ref.shape[-1])
        emit(cols, jnp.dot(lhs_ref[...], ref[...], preferred_element_type=_F32))
    out, c0 = [], 0
    for ref in w_tile_refs:
        out.append((ref.shape[-1], functools.partial(tile, c0, ref)))
        c0 += ref.shape[-1]
    return out


_PARAM_NAMES = ("vec", "sgu_w", "sgu_bias", "pool_w", "gate_w")
TILE_COLS = 2 * MXU_COLS
IN_TILE_WIDTHS = (TILE_COLS,) * (D_IN // TILE_COLS) + ((D_IN % TILE_COLS,) if D_IN % TILE_COLS else ())
OUT_TILE_WIDTHS = (MXU_COLS,) * (D_MODEL // MXU_COLS)
N_IN_TILES = len(IN_TILE_WIDTHS)
N_OUT_TILES = len(OUT_TILE_WIDTHS)
_STATE_NAMES = ("xb_ext", "s2_ext", "s4_ext", "s8_ext", "cx_halo", "h_carry", "seg_in", "seg_out")


def _layer_kernel(*refs, is_last, chunks_per_seq):
    x_ref, xlag_ref = refs[0], refs[1]
    n = 2 + len(_PARAM_NAMES)
    p = dict(zip(_PARAM_NAMES, refs[2:n]))
    w_in_tiles = refs[n:n + N_IN_TILES]
    w_out_tiles = refs[n + N_IN_TILES:n + N_IN_TILES + N_OUT_TILES]
    n += N_IN_TILES + N_OUT_TILES
    o_ref = refs[n]
    scratch = refs[n + 1:]
    proj_bufs = scratch[0:PROJ_SLOTS]
    y_bufs = scratch[PROJ_SLOTS:PROJ_SLOTS + STEP_CHUNKS]
    hn_buf = scratch[PROJ_SLOTS + STEP_CHUNKS]
    s = dict(zip(_STATE_NAMES, scratch[PROJ_SLOTS + STEP_CHUNKS + 1:]))
    k = pl.program_id(0)

    @pl.when(k == 0)
    def _init():
        for ref in scratch:
            ref[...] = jnp.zeros(ref.shape, ref.dtype)

    def chunk_slot(j, project, mix_it):
        rows = slice(j * TOK, (j + 1) * TOK)
        proj_ref = proj_bufs[j % PROJ_SLOTS]

        def emit_out(cols, t):
            o_ref[rows, cols] = xlag_ref[rows, cols] + t

        def emit_proj(cols, t):
            proj_ref[:, cols] = t

        mxu_work = _matmul_tiles(y_bufs[j], w_out_tiles, emit_out)
        if project:
            mxu_work += _matmul_tiles(hn_buf, w_in_tiles, emit_proj)
        mix = iter(())
        if mix_it:
            cm = lax.rem(STEP_CHUNKS * k + j + chunks_per_seq - 1, chunks_per_seq)
            mix = _mix_pieces(proj_bufs[(j - 1) % PROJ_SLOTS], y_bufs[(j - 1) % STEP_CHUNKS], cm, p, s)

        for t, (width, thunk) in enumerate(mxu_work):
            for part in range(width // MXU_COLS):
                if t == 0 and part == 0 and project:
                    hn_buf[...] = _rms_scale(x_ref[rows, :]).astype(_BF16)
                else:
                    next(mix, None)
            thunk()
            if is_last and t == N_OUT_TILES - 1:
                o_ref[rows, :] = _rmsnorm(o_ref[rows, :], _vec(p["vec"], "final_g"))
        for _ in mix:
            pass

    last_step = pl.num_programs(0) - 1

    @pl.when(k < last_step)
    def _steady():
        for j in range(STEP_CHUNKS):
            chunk_slot(j, project=True, mix_it=True)

    @pl.when(k == last_step)
    def _drain():
        for j in range(STEP_CHUNKS):
            chunk_slot(j, project=False, mix_it=(j == 0))


def _block_diag(w):
    n, d = w.shape[-3], w.shape[-2]
    eye = jnp.eye(n, dtype=w.dtype)
    out = eye[:, None, :, None] * w[..., :, :, None, :]
    return out.reshape(w.shape[:-3] + (n * d, n * d))


def _prepare_params(norm_g, w_in, sgu_norm_g, sgu_w, sgu_b, pool_w, pool_b, pool_scale, conv_w, conv_b,
                    lru_wa, lru_ba, lru_wx, lru_bx, lru_lambda, branch_norm_g, w_out, final_g):
    depth = norm_g.shape[0]
    f32 = lambda a: a.astype(_F32)
    hph = C_HEADS // 2
    both = jnp.stack([f32(lru_wa), f32(lru_wx)], axis=1)
    both = both.reshape(depth, 2, 2, hph, C_HEAD_DIM, C_HEAD_DIM)
    gate_w = jnp.einsum("lpdgio,gh->ldgipho", both, jnp.eye(hph, dtype=_F32))
    gate_w = gate_w.reshape(depth, 2, C_HALF, 2 * C_HALF)
    half_gate_cols = jnp.concatenate([
        jnp.ones((2 * A_WIDTH,)), jnp.full((A_WIDTH,), 0.5), jnp.ones((B_WIDTH,)), jnp.full((B_WIDTH,), 0.5),
        jnp.ones((C_WIDTH,)), jnp.full((C_WIDTH,), 0.5)]).astype(_F32)
    sgu_bias = jnp.repeat(jnp.swapaxes(sgu_b, 1, 2), A_HEAD_DIM, axis=2).astype(_F32)

    def vec_row(*parts):
        row = jnp.concatenate([f32(a) for a in parts], axis=-1)
        return jnp.pad(row, ((0, 0), (0, D_MODEL - row.shape[-1])))

    vec = jnp.stack(
        [vec_row(jnp.broadcast_to(final_g, (depth, D_MODEL))),
         vec_row(sgu_norm_g, pool_b * pool_scale),
         vec_row(0.5 * conv_b, lru_lambda),
         vec_row(0.5 * lru_ba, 0.5 * lru_bx)]
        + [vec_row(0.5 * conv_w[:, k]) for k in range(CONV_WIDTH)], axis=1)
    pool_w_scaled = _block_diag(f32(pool_w)) * f32(pool_scale)[:, None, :]
    small = dict(vec=vec, sgu_w=f32(sgu_w), sgu_bias=sgu_bias, pool_w=pool_w_scaled.astype(_BF16),
                 gate_w=gate_w.astype(_BF16))
    w_in_bf = (f32(w_in) * f32(norm_g)[:, :, None] * half_gate_cols).astype(_BF16)
    w_out_bf = (f32(w_out) * f32(branch_norm_g)[:, :, None]).astype(_BF16)
    return small, w_in_bf, w_out_bf


def _layer(x, small, w_in_bf, w_out_bf, layer, *, is_last, name):
    bsz, seq, _ = x.shape
    step_rows = STEP_CHUNKS * TOK
    assert seq % step_rows == 0 and TOK % SGU_BLOCK == 0 and STEP_CHUNKS % PROJ_SLOTS == 0
    steps_per_seq = seq // step_rows
    n_real = bsz * steps_per_seq
    small_ops = tuple(small[n] for n in _PARAM_NAMES)
    operands = (x, x) + small_ops + (w_in_bf,) * N_IN_TILES + (w_out_bf,) * N_OUT_TILES

    def x_map(k):
        kk = jnp.minimum(k, n_real - 1)
        return (kk // steps_per_seq, kk % steps_per_seq, 0)

    def lag_map(k):
        kl = jnp.maximum(k - LAG_STEPS, 0)
        return (kl // steps_per_seq, kl % steps_per_seq, 0)

    def of_layer(shape):
        return pl.BlockSpec((None,) + tuple(shape[1:]), lambda k: (layer,) + (0,) * (len(shape) - 1),
                            pipeline_mode=pl.Buffered(1))

    in_specs = [pl.BlockSpec((None, step_rows, D_MODEL), x_map),
                pl.BlockSpec((None, step_rows, D_MODEL), lag_map)]
    in_specs += [of_layer(op.shape) for op in small_ops]
    for w, widths in ((w_in_bf, IN_TILE_WIDTHS), (w_out_bf, OUT_TILE_WIDTHS)):
        c0 = 0
        for width in widths:
            assert c0 % width == 0
            in_specs.append(pl.BlockSpec((None, w.shape[1], width),
                                         functools.partial(lambda k, ci: (layer, 0, ci), ci=c0 // width),
                                         pipeline_mode=pl.Buffered(1)))
            c0 += width
    scratch_shapes = (
        [pltpu.VMEM((TOK, D_IN), _F32) for _ in range(PROJ_SLOTS)]
        + [pltpu.VMEM((TOK, D_MODEL), _BF16) for _ in range(STEP_CHUNKS)]
        + [pltpu.VMEM((TOK, D_MODEL), _BF16),
           pltpu.VMEM((TOK + POOL_HALO, B_WIDTH), _F32),
           pltpu.VMEM((TOK + POOL_HALO, B_WIDTH), _F32),
           pltpu.VMEM((TOK + POOL_HALO, LANES), _F32),
           pltpu.VMEM((TOK + POOL_HALO, LANES), _F32),
           pltpu.VMEM((CONV_HALO, C_WIDTH), _F32),
           pltpu.VMEM((SUBLANES, C_WIDTH), _F32),
           pltpu.VMEM((C_WIDTH // LANES, SEGS * SEG_PITCH, LANES), _F32),
           pltpu.VMEM((C_WIDTH // LANES, SEGS * SEG_PITCH, LANES), _F32)])
    return pl.pallas_call(
        functools.partial(_layer_kernel, is_last=is_last, chunks_per_seq=seq // TOK),
        out_shape=jax.ShapeDtypeStruct(x.shape, x.dtype),
        grid=(n_real + LAG_STEPS,),
        in_specs=in_specs,
        out_specs=pl.BlockSpec((None, step_rows, D_MODEL), lag_map),
        scratch_shapes=scratch_shapes,
        compiler_params=pltpu.CompilerParams(
            dimension_semantics=("arbitrary",),
            vmem_limit_bytes=VMEM_LIMIT_BYTES,
        ),
        name=name,
    )(*operands)


def kernel(x, norm_g, w_in, sgu_norm_g, sgu_w, sgu_b, pool_w, pool_b, pool_scale, conv_w, conv_b,
           lru_wa, lru_ba, lru_wx, lru_bx, lru_lambda, branch_norm_g, w_out, final_g):
    small, w_in_bf, w_out_bf = _prepare_params(
        norm_g, w_in, sgu_norm_g, sgu_w, sgu_b, pool_w, pool_b, pool_scale, conv_w, conv_b,
        lru_wa, lru_ba, lru_wx, lru_bx, lru_lambda, branch_norm_g, w_out, final_g)
    for l in range(DEPTH):
        x = _layer(x, small, w_in_bf, w_out_bf, l, is_last=(l == DEPTH - 1), name=f"hybrid_layer{l}")
    return x
```
